```python
import math
import jax
import jax.numpy as jnp
from jax import lax
import numpy as np

D_MODEL = 4096
BATCH = 2
SEQ = 4096
DEPTH = 2

F32 = jnp.float32
BRANCH_WIDTH = 1024
N_BRANCHES = 4
Q_BLOCK = 128
ROPE_THETA = 10000.0
LN_EPS = 1e-5
RMS_EPS = 1e-6
DEEPNORM_ALPHA = (2 * DEPTH) ** 0.25
DEEPNORM_BETA = (8 * DEPTH) ** -0.25

SSM_HEADS = 16
SSM_HEAD_DIM = 64
SSM_WIDTH = SSM_HEADS * SSM_HEAD_DIM
SSM_GROUPS = 4
SSM_STATE = 128
SSM_CONV = 4
SSM_CHUNK = 128
SSM_CONV_DIM = SSM_WIDTH + 2 * SSM_GROUPS * SSM_STATE

MOBA_HEADS = 8
MOBA_HEAD_DIM = 128
MOBA_BLOCK = 256
MOBA_TOPK = 3
MOBA_QCHUNK = 16

MLA_HEADS = 8
MLA_Q_RANK = 768
MLA_KV_RANK = 256
MLA_NOPE_DIM = 128
MLA_ROPE_DIM = 64
MLA_V_DIM = 128

DIFF_HEADS = 8
DIFF_HEAD_DIM = 64
DIFF_V_DIM = 2 * DIFF_HEAD_DIM

MOE_GROUPS = 4
MOE_EXPERTS_PER_GROUP = 8
MOE_TOPK = 2
MOE_D_FF = 512

IN_SPLITS = (SSM_WIDTH, SSM_CONV_DIM, SSM_HEADS,
             MOBA_HEADS * MOBA_HEAD_DIM, MOBA_HEADS * MOBA_HEAD_DIM, MOBA_HEADS * MOBA_HEAD_DIM,
             MLA_Q_RANK, MLA_KV_RANK, MLA_ROPE_DIM,
             2 * DIFF_HEADS * DIFF_HEAD_DIM, 2 * DIFF_HEADS * DIFF_HEAD_DIM, DIFF_HEADS * DIFF_V_DIM)
N_IN = sum(IN_SPLITS)

kernel_name = 'hybrid_ssd_moba_mla_diff_hiermoe_deepnorm'


def split_cols(u, sizes):
    idx = np.cumsum(np.array(sizes))[:-1].tolist()
    return jnp.split(u, idx, axis=-1)


def layer_norm(t, g, b):
    tf = t.astype(F32)
    mu = jnp.mean(tf, -1, keepdims=True)
    var = jnp.mean(jnp.square(tf - mu), -1, keepdims=True)
    return ((tf - mu) * lax.rsqrt(var + LN_EPS) * g.astype(F32) + b.astype(F32)).astype(t.dtype)


def rms_norm(t, g):
    tf = t.astype(F32)
    return (tf * lax.rsqrt(jnp.mean(tf * tf, -1, keepdims=True) + RMS_EPS) * g.astype(F32)).astype(t.dtype)


def rope_tables(seq, dim):
    inv = 1.0 / (ROPE_THETA ** (jnp.arange(0, dim, 2, dtype=F32) / dim))
    ang = jnp.arange(seq, dtype=F32)[:, None] * inv[None, :]
    return jnp.cos(ang), jnp.sin(ang)


def apply_rope(t, cos, sin):
    half = t.shape[-1] // 2
    t1, t2 = t[..., :half], t[..., half:]
    c = cos[:, None, :].astype(t.dtype)
    s = sin[:, None, :].astype(t.dtype)
    return jnp.concatenate([t1 * c - t2 * s, t1 * s + t2 * c], axis=-1)


def causal_depthwise_conv(u, w, b):
    k, c = w.shape
    out = lax.conv_general_dilated(u, w[:, None, :].astype(u.dtype), window_strides=(1,),
                                   padding=[(k - 1, 0)], dimension_numbers=('NWC', 'WIO', 'NWC'),
                                   feature_group_count=c)
    return out + b.astype(u.dtype)


def segsum(a):
    t = a.shape[-1]
    x = jnp.broadcast_to(a[..., :, None], a.shape + (t,))
    strict = jnp.tril(jnp.ones((t, t), dtype=bool), -1)
    xs = jnp.cumsum(jnp.where(strict, x, 0.0), axis=-2)
    return jnp.where(jnp.tril(jnp.ones((t, t), dtype=bool)), xs, -jnp.inf)


def ssd_chunked(xh, a, bm, cm):
    b, s, h, p = xh.shape
    g, n = bm.shape[2], bm.shape[3]
    r = h // g
    c = s // SSM_CHUNK
    x = xh.reshape(b, c, SSM_CHUNK, g, r, p)
    bc = bm.reshape(b, c, SSM_CHUNK, g, n)
    cc = cm.reshape(b, c, SSM_CHUNK, g, n)
    a = a.reshape(b, c, SSM_CHUNK, g, r).transpose(0, 3, 4, 1, 2)
    a_cs = jnp.cumsum(a, axis=-1)
    lmat = jnp.exp(segsum(a))
    y_diag = jnp.einsum('bclgn,bcsgn,bgrcls,bcsgrp->bclgrp', cc, bc, lmat, x)
    decay_states = jnp.exp(a_cs[..., -1:] - a_cs)
    states = jnp.einsum('bclgn,bgrcl,bclgrp->bcgrpn', bc, decay_states, x)
    states = jnp.concatenate([jnp.zeros_like(states[:, :1]), states], axis=1)
    chunk_a = jnp.pad(a_cs[..., -1], ((0, 0), (0, 0), (0, 0), (1, 0)))
    decay_chunk = jnp.exp(segsum(chunk_a))
    states = jnp.einsum('bgrzc,bcgrpn->bzgrpn', decay_chunk, states)[:, :-1]
    y_off = jnp.einsum('bclgn,bcgrpn,bgrcl->bclgrp', cc, states, jnp.exp(a_cs))
    return (y_diag + y_off).reshape(b, s, h, p)


def mamba2_branch(z, xbc, dt_raw, conv_w, conv_b, dt_bias, a_log, d_skip, norm_g):
    b, s, _ = z.shape
    xbc = jax.nn.silu(causal_depthwise_conv(xbc, conv_w, conv_b))
    xs, bm, cm = split_cols(xbc, (SSM_WIDTH, SSM_GROUPS * SSM_STATE, SSM_GROUPS * SSM_STATE))
    xs = xs.reshape(b, s, SSM_HEADS, SSM_HEAD_DIM).astype(F32)
    bm = bm.reshape(b, s, SSM_GROUPS, SSM_STATE).astype(F32)
    cm = cm.reshape(b, s, SSM_GROUPS, SSM_STATE).astype(F32)
    dt = jax.nn.softplus(dt_raw.astype(F32) + dt_bias.astype(F32))
    a = -jnp.exp(a_log.astype(F32))
    y = ssd_chunked(xs * dt[..., None], dt * a, bm, cm)
    y = (y + d_skip.astype(F32)[:, None] * xs).reshape(b, s, SSM_WIDTH)
    return rms_norm(y * jax.nn.silu(z.astype(F32)), norm_g).astype(z.dtype)


def moba_attention(q, k, v):
    b, s, h, d = q.shape
    nb = -(-s // MOBA_BLOCK)
    sp = nb * MOBA_BLOCK
    pad = ((0, 0), (0, sp - s), (0, 0), (0, 0))
    q = jnp.pad(q, pad)
    k = jnp.pad(k, pad)
    v = jnp.pad(v, pad)
    scale = d ** -0.5
    kb = k.reshape(b, nb, MOBA_BLOCK, h, d).transpose(0, 3, 1, 2, 4)
    vb = v.reshape(b, nb, MOBA_BLOCK, h, d).transpose(0, 3, 1, 2, 4)
    kmean = jnp.mean(kb.astype(F32), axis=3)
    q_blk = jnp.arange(sp) // MOBA_BLOCK
    gate = jnp.einsum('bshd,bhnd->bhsn', q.astype(F32), kmean)
    past = jnp.arange(nb)[None, :] < q_blk[:, None]
    gate = jnp.where(past, gate, -jnp.inf)
    topk = min(MOBA_TOPK, nb)
    _, sel = lax.top_k(gate, topk)
    valid = sel < q_blk[:, None]
    nc = sp // MOBA_QCHUNK
    qc = q.reshape(b, nc, MOBA_QCHUNK, h, d).transpose(1, 0, 3, 2, 4)
    selc = sel.reshape(b, h, nc, MOBA_QCHUNK, topk).transpose(2, 0, 1, 3, 4)
    validc = valid.reshape(b, h, nc, MOBA_QCHUNK, topk).transpose(2, 0, 1, 3, 4)
    bi = jnp.arange(b)[:, None, None, None]
    hi = jnp.arange(h)[None, :, None, None]

    def one_chunk(args):
        i, qi, si, vmask = args
        k_sel = kb[bi, hi, si]
        v_sel = vb[bi, hi, si]
        s_sel = jnp.einsum('bhqd,bhqnkd->bhqnk', qi, k_sel).astype(F32) * scale
        s_sel = jnp.where(vmask[..., None], s_sel, -jnp.inf).reshape(b, h, MOBA_QCHUNK, topk * MOBA_BLOCK)
        own = (i * MOBA_QCHUNK) // MOBA_BLOCK
        k_own = lax.dynamic_index_in_dim(kb, own, axis=2, keepdims=False)
        v_own = lax.dynamic_index_in_dim(vb, own, axis=2, keepdims=False)
        s_own = jnp.einsum('bhqd,bhkd->bhqk', qi, k_own).astype(F32) * scale
        qpos = i * MOBA_QCHUNK + jnp.arange(MOBA_QCHUNK)
        kpos = own * MOBA_BLOCK + jnp.arange(MOBA_BLOCK)
        s_own = jnp.where(kpos[None, :] <= qpos[:, None], s_own, -jnp.inf)
        p = jax.nn.softmax(jnp.concatenate([s_sel, s_own], axis=-1), axis=-1).astype(v.dtype)
        p_sel = p[..., :topk * MOBA_BLOCK].reshape(b, h, MOBA_QCHUNK, topk, MOBA_BLOCK)
        p_own = p[..., topk * MOBA_BLOCK:]
        return (jnp.einsum('bhqnk,bhqnkd->bqhd', p_sel, v_sel)
                + jnp.einsum('bhqk,bhkd->bqhd', p_own, v_own))

    out = lax.map(one_chunk, (jnp.arange(nc), qc, selc, validc))
    out = out.transpose(1, 0, 2, 3, 4).reshape(b, sp, h, d)
    return out[:, :s]


def moba_branch(q, k, v, cos, sin):
    b, s, _ = q.shape
    q = apply_rope(q.reshape(b, s, MOBA_HEADS, MOBA_HEAD_DIM), cos, sin)
    k = apply_rope(k.reshape(b, s, MOBA_HEADS, MOBA_HEAD_DIM), cos, sin)
    v = v.reshape(b, s, MOBA_HEADS, MOBA_HEAD_DIM)
    return moba_attention(q, k, v).reshape(b, s, BRANCH_WIDTH)


def causal_softmax_block(i, qi, k, scale):
    qb = qi.shape[1]
    s = jnp.einsum('bqhd,bkhd->bhqk', qi, k).astype(F32) * scale
    qpos = i * qb + jnp.arange(qb)
    mask = jnp.arange(k.shape[1])[None, :] <= qpos[:, None]
    return jax.nn.softmax(jnp.where(mask, s, -jnp.inf), axis=-1)


def sweep_query_blocks(fn, *qs):
    b, s = qs[0].shape[0], qs[0].shape[1]
    nq = s // Q_BLOCK
    blocks = tuple(jnp.moveaxis(t.reshape((b, nq, Q_BLOCK) + t.shape[2:]), 1, 0) for t in qs)
    out = lax.map(lambda a: fn(*a), (jnp.arange(nq),) + blocks)
    out = jnp.moveaxis(out, 0, 1)
    return out.reshape((b, s) + out.shape[3:])


def mla_branch(c_q, c_kv, k_rope, q_norm_g, kv_norm_g, w_uq, w_ukv, cos, sin):
    b, s, _ = c_q.shape
    q = (rms_norm(c_q, q_norm_g) @ w_uq).reshape(b, s, MLA_HEADS, MLA_NOPE_DIM + MLA_ROPE_DIM)
    q_nope, q_pe = q[..., :MLA_NOPE_DIM], q[..., MLA_NOPE_DIM:]
    kv = (rms_norm(c_kv, kv_norm_g) @ w_ukv).reshape(b, s, MLA_HEADS, MLA_NOPE_DIM + MLA_V_DIM)
    k_nope, v = kv[..., :MLA_NOPE_DIM], kv[..., MLA_NOPE_DIM:]
    k_pe = apply_rope(k_rope[:, :, None, :], cos, sin)
    q = jnp.concatenate([q_nope, apply_rope(q_pe, cos, sin)], axis=-1)
    k = jnp.concatenate([k_nope, jnp.broadcast_to(k_pe, (b, s, MLA_HEADS, MLA_ROPE_DIM))], axis=-1)
    scale = (MLA_NOPE_DIM + MLA_ROPE_DIM) ** -0.5

    def block(i, qi):
        p = causal_softmax_block(i, qi, k, scale)
        return jnp.einsum('bhqk,bkhd->bqhd', p.astype(v.dtype), v)

    return sweep_query_blocks(block, q).reshape(b, s, BRANCH_WIDTH)


def diff_branch(q, k, v, lq1, lk1, lq2, lk2, subln_g, lam_init, cos, sin):
    b, s, _ = q.shape
    q = q.reshape(b, s, DIFF_HEADS, 2, DIFF_HEAD_DIM)
    k = k.reshape(b, s, DIFF_HEADS, 2, DIFF_HEAD_DIM)
    v = v.reshape(b, s, DIFF_HEADS, DIFF_V_DIM)
    q1 = apply_rope(q[..., 0, :], cos, sin)
    q2 = apply_rope(q[..., 1, :], cos, sin)
    k1 = apply_rope(k[..., 0, :], cos, sin)
    k2 = apply_rope(k[..., 1, :], cos, sin)
    lam = (jnp.exp(jnp.sum(lq1.astype(F32) * lk1.astype(F32)))
           - jnp.exp(jnp.sum(lq2.astype(F32) * lk2.astype(F32))) + lam_init)
    scale = DIFF_HEAD_DIM ** -0.5

    def block(i, q1i, q2i):
        p = causal_softmax_block(i, q1i, k1, scale) - lam * causal_softmax_block(i, q2i, k2, scale)
        return jnp.einsum('bhqk,bkhd->bqhd', p.astype(v.dtype), v)

    o = sweep_query_blocks(block, q1, q2)
    o = rms_norm(o, subln_g) * (1.0 - lam_init)
    return o.reshape(b, s, BRANCH_WIDTH)


def hier_moe(h, w_rg, w_re, w_g, w_u, w_d):
    b, s, d = h.shape
    t = h.reshape(b * s, d)
    g_logits = (t @ w_rg).astype(F32)
    g_prob = jax.nn.softmax(g_logits, axis=-1)
    g_sel = jnp.argmax(g_logits, axis=-1)
    g_w = jnp.take_along_axis(g_prob, g_sel[:, None], axis=1)
    e_logits = (t @ w_re).astype(F32).reshape(-1, MOE_GROUPS, MOE_EXPERTS_PER_GROUP)
    e_logits = jnp.take_along_axis(e_logits, g_sel[:, None, None], axis=1)[:, 0]
    top_p, top_i = lax.top_k(jax.nn.softmax(e_logits, axis=-1), MOE_TOPK)
    top_p = top_p / jnp.sum(top_p, axis=-1, keepdims=True)
    within = jnp.einsum('tk,tke->te', top_p, jax.nn.one_hot(top_i, MOE_EXPERTS_PER_GROUP, dtype=F32))
    combine = (g_w[:, :, None] * jax.nn.one_hot(g_sel, MOE_GROUPS, dtype=F32)[:, :, None]
               * within[:, None, :]).astype(h.dtype)
    out = jnp.zeros_like(t)
    for g in range(MOE_GROUPS):
        a = jnp.einsum('td,edf->tef', t, w_g[g])
        u = jnp.einsum('td,edf->tef', t, w_u[g])
        hid = jax.nn.silu(a) * u * combine[:, g, :, None]
        out = out + jnp.einsum('tef,efd->td', hid, w_d[g])
    return out.reshape(b, s, d)


def setup_inputs(seed: int = 0) -> dict:
    key = jax.random.key(seed)
    ks = iter(jax.random.split(key, 48))
    L, D = DEPTH, D_MODEL

    def nrm(shape, scale):
        return jax.random.normal(next(ks), shape, F32) * scale

    x = nrm((BATCH, SEQ, D), 1.0)
    emb_ln_g = 1.0 + nrm((D,), 0.02)
    emb_ln_b = nrm((D,), 0.02)
    w_in = nrm((L, D, N_IN), D ** -0.5)
    conv_w = nrm((L, SSM_CONV, SSM_CONV_DIM), SSM_CONV ** -0.5)
    conv_b = nrm((L, SSM_CONV_DIM), 0.02)
    dt0 = jnp.exp(jax.random.uniform(next(ks), (L, SSM_HEADS), F32, math.log(1e-3), math.log(1e-1)))
    dt_bias = dt0 + jnp.log(-jnp.expm1(-dt0))
    a_log = jnp.log(jax.random.uniform(next(ks), (L, SSM_HEADS), F32, 1.0, 16.0))
    d_skip = 1.0 + nrm((L, SSM_HEADS), 0.02)
    ssm_norm_g = 1.0 + nrm((L, SSM_WIDTH), 0.02)
    mla_q_norm_g = 1.0 + nrm((L, MLA_Q_RANK), 0.02)
    mla_kv_norm_g = 1.0 + nrm((L, MLA_KV_RANK), 0.02)
    mla_w_uq = nrm((L, MLA_Q_RANK, MLA_HEADS * (MLA_NOPE_DIM + MLA_ROPE_DIM)), MLA_Q_RANK ** -0.5)
    mla_w_ukv = nrm((L, MLA_KV_RANK, MLA_HEADS * (MLA_NOPE_DIM + MLA_V_DIM)), MLA_KV_RANK ** -0.5)
    diff_lambda_q1 = nrm((L, DIFF_HEAD_DIM), 0.1)
    diff_lambda_k1 = nrm((L, DIFF_HEAD_DIM), 0.1)
    diff_lambda_q2 = nrm((L, DIFF_HEAD_DIM), 0.1)
    diff_lambda_k2 = nrm((L, DIFF_HEAD_DIM), 0.1)
    diff_subln_g = 1.0 + nrm((L, DIFF_V_DIM), 0.02)
    w_gate = nrm((L, N_BRANCHES, D, D), D ** -0.5)
    b_gate = nrm((L, N_BRANCHES, D), 0.02)
    w_branch = nrm((L, N_BRANCHES, BRANCH_WIDTH, D), BRANCH_WIDTH ** -0.5 * DEEPNORM_BETA)
    w_out = nrm((L, D, D), D ** -0.5 * DEEPNORM_BETA)
    ln1_g = 1.0 + nrm((L, D), 0.02)
    ln1_b = nrm((L, D), 0.02)
    router_group = nrm((L, D, MOE_GROUPS), D ** -0.5)
    router_expert = nrm((L, D, MOE_GROUPS * MOE_EXPERTS_PER_GROUP), D ** -0.5)
    moe_w_gate = nrm((L, MOE_GROUPS, MOE_EXPERTS_PER_GROUP, D, MOE_D_FF), D ** -0.5)
    moe_w_up = nrm((L, MOE_GROUPS, MOE_EXPERTS_PER_GROUP, D, MOE_D_FF), D ** -0.5)
    moe_w_down = nrm((L, MOE_GROUPS, MOE_EXPERTS_PER_GROUP, MOE_D_FF, D), MOE_D_FF ** -0.5 * DEEPNORM_BETA)
    ln2_g = 1.0 + nrm((L, D), 0.02)
    ln2_b = nrm((L, D), 0.02)
    return {'x': x, 'emb_ln_g': emb_ln_g, 'emb_ln_b': emb_ln_b, 'w_in': w_in,
            'conv_w': conv_w, 'conv_b': conv_b, 'dt_bias': dt_bias, 'a_log': a_log, 'd_skip': d_skip,
            'ssm_norm_g': ssm_norm_g, 'mla_q_norm_g': mla_q_norm_g, 'mla_kv_norm_g': mla_kv_norm_g,
            'mla_w_uq': mla_w_uq, 'mla_w_ukv': mla_w_ukv,
            'diff_lambda_q1': diff_lambda_q1, 'diff_lambda_k1': diff_lambda_k1,
            'diff_lambda_q2': diff_lambda_q2, 'diff_lambda_k2': diff_lambda_k2, 'diff_subln_g': diff_subln_g,
            'w_gate': w_gate, 'b_gate': b_gate, 'w_branch': w_branch, 'w_out': w_out,
            'ln1_g': ln1_g, 'ln1_b': ln1_b, 'router_group': router_group, 'router_expert': router_expert,
            'moe_w_gate': moe_w_gate, 'moe_w_up': moe_w_up, 'moe_w_down': moe_w_down,
            'ln2_g': ln2_g, 'ln2_b': ln2_b}


def reference(x, emb_ln_g, emb_ln_b, w_in, conv_w, conv_b, dt_bias, a_log, d_skip, ssm_norm_g,
              mla_q_norm_g, mla_kv_norm_g, mla_w_uq, mla_w_ukv,
              diff_lambda_q1, diff_lambda_k1, diff_lambda_q2, diff_lambda_k2, diff_subln_g,
              w_gate, b_gate, w_branch, w_out, ln1_g, ln1_b, router_group, router_expert,
              moe_w_gate, moe_w_up, moe_w_down, ln2_g, ln2_b):
    s = x.shape[1]
    cos_128, sin_128 = rope_tables(s, MOBA_HEAD_DIM)
    cos_64, sin_64 = rope_tables(s, DIFF_HEAD_DIM)
    h = layer_norm(x, emb_ln_g, emb_ln_b)
    for l in range(DEPTH):
        u = h @ w_in[l]
        (z, xbc, dt_raw, mq, mk, mv, c_q, c_kv, k_rope, dq, dk, dv) = split_cols(u, IN_SPLITS)
        y_ssd = mamba2_branch(z, xbc, dt_raw, conv_w[l], conv_b[l], dt_bias[l], a_log[l], d_skip[l], ssm_norm_g[l])
        y_moba = moba_branch(mq, mk, mv, cos_128, sin_128)
        y_mla = mla_branch(c_q, c_kv, k_rope, mla_q_norm_g[l], mla_kv_norm_g[l], mla_w_uq[l], mla_w_ukv[l], cos_64, sin_64)
        lam_init = 0.8 - 0.6 * math.exp(-0.3 * l)
        y_diff = diff_branch(dq, dk, dv, diff_lambda_q1[l], diff_lambda_k1[l], diff_lambda_q2[l],
                             diff_lambda_k2[l], diff_subln_g[l], lam_init, cos_64, sin_64)
        merged = jnp.zeros_like(h)
        for i, y in enumerate((y_ssd, y_moba, y_mla, y_diff)):
            gate = jax.nn.sigmoid(h @ w_gate[l, i] + b_gate[l, i])
            merged = merged + gate * (y @ w_branch[l, i])
        h = layer_norm(DEEPNORM_ALPHA * h + merged @ w_out[l], ln1_g[l], ln1_b[l])
        ffn = hier_moe(h, router_group[l], router_expert[l], moe_w_gate[l], moe_w_up[l], moe_w_down[l])
        h = layer_norm(DEEPNORM_ALPHA * h + ffn, ln2_g[l], ln2_b[l])
    return h
```

```python
import functools
import math

import jax
import jax.numpy as jnp
import numpy as np
from jax import lax
from jax.experimental import pallas as pl
from jax.experimental.pallas import tpu as pltpu

F32 = jnp.float32
BF16 = jnp.bfloat16
I32 = jnp.int32
HI = lax.Precision.HIGHEST

LANES = 128
SUBLANES = 8
MIB = 1024 * 1024

D_MODEL = 4096
DEPTH = 2
BRANCH_WIDTH = 1024
N_BRANCHES = 4
ROPE_THETA = 10000.0
LN_EPS = 1e-5
RMS_EPS = 1e-6
DEEPNORM_ALPHA = (2 * DEPTH) ** 0.25

SSM_HEADS = 16
SSM_HEAD_DIM = 64
SSM_WIDTH = SSM_HEADS * SSM_HEAD_DIM
SSM_GROUPS = 4
SSM_STATE = 128
SSM_CONV = 4
SSM_CHUNK = 128
SSM_CONV_DIM = SSM_WIDTH + 2 * SSM_GROUPS * SSM_STATE

MOBA_HEADS = 8
MOBA_HEAD_DIM = 128
MOBA_BLOCK = 256
MOBA_TOPK = 3

MLA_HEADS = 8
MLA_Q_RANK = 768
MLA_KV_RANK = 256
MLA_NOPE_DIM = 128
MLA_ROPE_DIM = 64
MLA_V_DIM = 128

DIFF_HEADS = 8
DIFF_HEAD_DIM = 64
DIFF_V_DIM = 2 * DIFF_HEAD_DIM

MOE_GROUPS = 4
MOE_EXPERTS_PER_GROUP = 8
MOE_EXPERTS = MOE_GROUPS * MOE_EXPERTS_PER_GROUP
MOE_TOPK = 2
MOE_D_FF = 512

U_MAIN = 10240
COL_Z, COL_X, COL_BC = 0, 1024, 2048
COL_MQ, COL_MK, COL_MV = 3072, 4096, 5120
COL_CQ, COL_CKV = 6144, 6912
COL_DQ, COL_DK, COL_DV = 7168, 8192, 9216
TAIL_DT = 64

ATT_TILE = 512
ROW_TILE = 256
MOE_TILE = 256
NEG = -1e30


def _cparams(sem, vmem_mib):
    return pltpu.CompilerParams(dimension_semantics=sem, vmem_limit_bytes=vmem_mib * MIB)


def _silu(x):
    return x * (1.0 / (1.0 + jnp.exp(-x)))


def _sigmoid(x):
    return 1.0 / (1.0 + jnp.exp(-x))


def _rope128(t, cos_f, sin_s):
    return t * cos_f + pltpu.roll(t, 64, 1) * sin_s


def _rope64x2(t, cos_f, sin_a, sin_b):
    return t * cos_f + pltpu.roll(t, 96, 1) * sin_a + pltpu.roll(t, 32, 1) * sin_b


def _ln_rows(x, g, b):
    mu = jnp.mean(x, axis=-1, keepdims=True)
    xc = x - mu
    var = jnp.mean(xc * xc, axis=-1, keepdims=True)
    return xc * lax.rsqrt(var + LN_EPS) * g + b


def _ln_kernel(x_ref, g_ref, b_ref, o32_ref, o16_ref):
    y = _ln_rows(x_ref[...], g_ref[...], b_ref[...])
    o32_ref[...] = y
    o16_ref[...] = y.astype(BF16)


def _res_ln_kernel(h_ref, m_ref, g_ref, b_ref, o32_ref, o16_ref):
    y = _ln_rows(DEEPNORM_ALPHA * h_ref[...] + m_ref[...], g_ref[...], b_ref[...])
    o32_ref[...] = y
    o16_ref[...] = y.astype(BF16)


def _layer_norm(x, g, b, residual=None):
    t, d = x.shape
    row = pl.BlockSpec((ROW_TILE, d), lambda i: (i, 0))
    vec = pl.BlockSpec((1, d), lambda i: (0, 0))
    ins = [x] if residual is None else [residual, x]
    return pl.pallas_call(
        _ln_kernel if residual is None else _res_ln_kernel,
        grid=(t // ROW_TILE,),
        in_specs=[row] * len(ins) + [vec, vec],
        out_specs=[row, row],
        out_shape=[jax.ShapeDtypeStruct((t, d), F32), jax.ShapeDtypeStruct((t, d), BF16)],
        compiler_params=_cparams(("parallel",), 48),
        name="layer_norm" if residual is None else "residual_layer_norm",
    )(*ins, g.reshape(1, d), b.reshape(1, d))


def _mm_kernel(a_ref, b_ref, o_ref):
    o_ref[...] = jnp.dot(a_ref[...], b_ref[...], preferred_element_type=F32).astype(o_ref.dtype)


def _matmul(a, b, out_dtype, tm, tn, name):
    m, k = a.shape
    n = b.shape[1]
    tm, tn = min(tm, m), min(tn, n)
    assert m % tm == 0 and n % tn == 0
    return pl.pallas_call(
        _mm_kernel,
        grid=(n // tn, m // tm),
        in_specs=[pl.BlockSpec((tm, k), lambda j, i: (i, 0)),
                  pl.BlockSpec((k, tn), lambda j, i: (0, j))],
        out_specs=pl.BlockSpec((tm, tn), lambda j, i: (i, j)),
        out_shape=jax.ShapeDtypeStruct((m, n), out_dtype),
        compiler_params=_cparams(("parallel", "parallel"), 48),
        name=name,
    )(a, b)


def _ssd_kernel(z_ref, x_ref, bc_ref, tail_ref, cw_ref, cb_ref, dtb_ref, aneg_ref, dsk_ref, ng_ref,
                o_ref, extx_ref, extbc_ref, state_ref):
    c = pl.program_id(1)
    L = SSM_CHUNK

    @pl.when(c == 0)
    def _():
        extx_ref[0:SUBLANES, :] = jnp.zeros((SUBLANES, SSM_WIDTH), F32)
        extbc_ref[0:SUBLANES, :] = jnp.zeros((SUBLANES, SSM_WIDTH), F32)
        state_ref[...] = jnp.zeros(state_ref.shape, F32)

    def conv_silu(cur_ref, ext_ref, w, b):
        ext_ref[SUBLANES:SUBLANES + L, :] = cur_ref[...]
        acc = b
        for k in range(SSM_CONV):
            off = SUBLANES - (SSM_CONV - 1) + k
            acc = acc + w[k:k + 1, :] * ext_ref[off:off + L, :]
        ext_ref[0:SUBLANES, :] = ext_ref[L:L + SUBLANES, :]
        return _silu(acc)

    cw = cw_ref[...]
    cb = cb_ref[...]
    xs = conv_silu(x_ref, extx_ref, cw[:, :SSM_WIDTH], cb[:, :SSM_WIDTH])
    bc = conv_silu(bc_ref, extbc_ref, cw[:, SSM_WIDTH:], cb[:, SSM_WIDTH:])
    gn = SSM_GROUPS * SSM_STATE

    dtr = tail_ref[...] + dtb_ref[...]
    dt = jnp.maximum(dtr, 0.0) + jnp.log1p(jnp.exp(-jnp.abs(dtr)))
    a = dt * aneg_ref[...]
    ri = lax.broadcasted_iota(I32, (L, L), 0)
    ci = lax.broadcasted_iota(I32, (L, L), 1)
    tril = ri >= ci
    tril_f = tril.astype(F32)
    a_cs = jnp.dot(tril_f, a, precision=HI, preferred_element_type=F32)
    a_cs_t = lax.dot_general(a, (ci >= ri).astype(F32), (((0,), (0,)), ((), ())),
                             precision=HI, preferred_element_type=F32)
    er = lax.broadcasted_iota(I32, (LANES, SSM_WIDTH), 0)
    ec = lax.broadcasted_iota(I32, (LANES, SSM_WIDTH), 1)
    expand = (er - TAIL_DT == ec // SSM_HEAD_DIM).astype(F32)
    dt_e = jnp.dot(dt, expand, precision=HI, preferred_element_type=F32)
    acs_e = jnp.dot(a_cs, expand, precision=HI, preferred_element_type=F32)
    tot_e = acs_e[L - 1:L, :]
    ecs_e = jnp.exp(acs_e)
    etot_e = jnp.exp(tot_e)
    xdt = xs * dt_e
    xd = xdt * jnp.exp(tot_e - acs_e)

    lane = lax.broadcasted_iota(I32, (L, LANES), 1)
    left = lane < SSM_HEAD_DIM
    ys = []
    for g in range(SSM_GROUPS):
        bg = bc[:, g * SSM_STATE:(g + 1) * SSM_STATE].astype(BF16)
        cg = bc[:, gn + g * SSM_STATE:gn + (g + 1) * SSM_STATE].astype(BF16)
        gmat = lax.dot_general(cg, bg, (((1,), (1,)), ((), ())), preferred_element_type=F32)
        for pp in range(2):
            p = 2 * g + pp
            sl = slice(p * LANES, (p + 1) * LANES)
            xdt_p = xdt[:, sl].astype(BF16)
            yh = []
            for h in (2 * p, 2 * p + 1):
                col = a_cs[:, TAIL_DT + h:TAIL_DT + h + 1]
                row = a_cs_t[TAIL_DT + h:TAIL_DT + h + 1, :]
                lmat = jnp.exp(jnp.where(tril, col - row, -jnp.inf))
                yh.append(jnp.dot((gmat * lmat).astype(BF16), xdt_p, preferred_element_type=F32))
            y_diag = jnp.where(left, yh[0], yh[1])
            st = state_ref[p]
            y_off = jnp.dot(cg, st.astype(BF16), preferred_element_type=F32) * ecs_e[:, sl]
            new = lax.dot_general(bg, xd[:, sl].astype(BF16), (((0,), (0,)), ((), ())),
                                  preferred_element_type=F32)
            state_ref[p] = etot_e[:, sl] * st + new
            ys.append(y_diag + y_off + dsk_ref[:, sl] * xs[:, sl])
    y = jnp.concatenate(ys, axis=1) * _silu(z_ref[...])
    ms = jnp.mean(y * y, axis=-1, keepdims=True)
    o_ref[...] = (y * lax.rsqrt(ms + RMS_EPS) * ng_ref[...]).astype(BF16)


def _ssd_branch(u_main, u_tail, batch, seq, conv_w, conv_b, dt_bias, a_log, d_skip, norm_g):
    nc = seq // SSM_CHUNK
    L = SSM_CHUNK

    def lanes16(v):
        return jnp.zeros((1, LANES), F32).at[0, TAIL_DT:TAIL_DT + SSM_HEADS].set(v.astype(F32))

    def col(blk):
        return pl.BlockSpec((L, SSM_WIDTH), lambda b, c: (b * nc + c, blk))

    def vec(n):
        return pl.BlockSpec((1, n), lambda b, c: (0, 0))

    return pl.pallas_call(
        _ssd_kernel,
        grid=(batch, nc),
        in_specs=[col(COL_Z // SSM_WIDTH), col(COL_X // SSM_WIDTH), col(COL_BC // SSM_WIDTH),
                  pl.BlockSpec((L, LANES), lambda b, c: (b * nc + c, 0)),
                  pl.BlockSpec((SSM_CONV, SSM_CONV_DIM), lambda b, c: (0, 0)),
                  vec(SSM_CONV_DIM), vec(LANES), vec(LANES), vec(SSM_WIDTH), vec(SSM_WIDTH)],
        out_specs=pl.BlockSpec((L, SSM_WIDTH), lambda b, c: (b * nc + c, 0)),
        out_shape=jax.ShapeDtypeStruct((batch * seq, SSM_WIDTH), BF16),
        scratch_shapes=[pltpu.VMEM((L + SUBLANES, SSM_WIDTH), F32),
                        pltpu.VMEM((L + SUBLANES, SSM_WIDTH), F32),
                        pltpu.VMEM((SSM_HEADS // 2, SSM_STATE, LANES), F32)],
        compiler_params=_cparams(("parallel", "arbitrary"), 40),
        name="ssd_branch",
    )(u_main, u_main, u_main, u_tail, conv_w, conv_b.reshape(1, -1), lanes16(dt_bias),
      lanes16(-jnp.exp(a_log.astype(F32))), jnp.repeat(d_skip.astype(F32), SSM_HEAD_DIM).reshape(1, -1),
      norm_g.reshape(1, -1))


def _rms_rows(x, g):
    return x * lax.rsqrt(jnp.mean(x * x, axis=-1, keepdims=True) + RMS_EPS) * g


def _kv_prep_kernel(mk_ref, mv_ref, cq_ref, ckv_ref, dk_ref, dv_ref, tail_ref,
                    c128_ref, s128_ref, c64_ref, sa_ref, sb_ref, qg_ref, kvg_ref,
                    mk16_ref, kmean_ref, mv16_ref, cqn_ref, ckvn_ref, kpe_ref, dk16_ref, dv16_ref):
    c128, s128 = c128_ref[...], s128_ref[...]
    c64, sa, sb = c64_ref[...], sa_ref[...], sb_ref[...]
    means = []
    for h in range(MOBA_HEADS):
        sl = slice(h * LANES, (h + 1) * LANES)
        r = _rope128(mk_ref[:, sl], c128, s128)
        mk16_ref[:, sl] = r.astype(BF16)
        means.append(jnp.mean(r, axis=0, keepdims=True))
    kmean_ref[0] = jnp.concatenate(means, axis=0)
    mv16_ref[...] = mv_ref[...].astype(BF16)
    cqn_ref[...] = _rms_rows(cq_ref[...], qg_ref[...]).astype(BF16)
    ckvn_ref[...] = _rms_rows(ckv_ref[...], kvg_ref[...]).astype(BF16)
    r = _rope64x2(tail_ref[...], c64, sa, sb)
    lane = lax.broadcasted_iota(I32, r.shape, 1)
    kpe_ref[...] = jnp.where(lane < MLA_ROPE_DIM, r, pltpu.roll(r, 64, 1)).astype(BF16)
    for h in range(DIFF_HEADS):
        sl = slice(h * LANES, (h + 1) * LANES)
        dk16_ref[:, sl] = _rope64x2(dk_ref[:, sl], c64, sa, sb).astype(BF16)
    dv16_ref[...] = dv_ref[...].astype(BF16)


def _kv_prep(u_main, u_tail, seq, tabs, q_norm_g, kv_norm_g):
    t = u_main.shape[0]
    r = ROW_TILE
    assert r == MOBA_BLOCK
    ns = seq // r

    def col(width, start):
        return pl.BlockSpec((r, width), lambda i: (i, start // width))

    def tab():
        return pl.BlockSpec((r, LANES), lambda i: (i % ns, 0))

    def vec(n):
        return pl.BlockSpec((1, n), lambda i: (0, 0))

    def out(width):
        return pl.BlockSpec((r, width), lambda i: (i, 0))

    return pl.pallas_call(
        _kv_prep_kernel,
        grid=(t // r,),
        in_specs=[col(1024, COL_MK), col(1024, COL_MV), col(MLA_Q_RANK, COL_CQ), col(MLA_KV_RANK, COL_CKV),
                  col(1024, COL_DK), col(1024, COL_DV), pl.BlockSpec((r, LANES), lambda i: (i, 0)),
                  tab(), tab(), tab(), tab(), tab(), vec(MLA_Q_RANK), vec(MLA_KV_RANK)],
        out_specs=[out(1024), pl.BlockSpec((1, MOBA_HEADS, LANES), lambda i: (i, 0, 0)), out(1024),
                   out(MLA_Q_RANK), out(MLA_KV_RANK), out(LANES), out(1024), out(1024)],
        out_shape=[jax.ShapeDtypeStruct((t, 1024), BF16),
                   jax.ShapeDtypeStruct((t // r, MOBA_HEADS, LANES), F32),
                   jax.ShapeDtypeStruct((t, 1024), BF16),
                   jax.ShapeDtypeStruct((t, MLA_Q_RANK), BF16),
                   jax.ShapeDtypeStruct((t, MLA_KV_RANK), BF16),
                   jax.ShapeDtypeStruct((t, LANES), BF16),
                   jax.ShapeDtypeStruct((t, 1024), BF16),
                   jax.ShapeDtypeStruct((t, 1024), BF16)],
        compiler_params=_cparams(("parallel",), 40),
        name="kv_prep",
    )(u_main, u_main, u_main, u_main, u_main, u_main, u_tail,
      tabs["c128"], tabs["s128"], tabs["c64"], tabs["sa"], tabs["sb"],
      q_norm_g.reshape(1, -1), kv_norm_g.reshape(1, -1))


def _online_step(s, v, m_ref, l_ref, acc_ref):
    m_prev = m_ref[...]
    m_new = jnp.maximum(m_prev, jnp.max(s, axis=-1, keepdims=True))
    alpha = jnp.exp(m_prev - m_new)
    p = jnp.exp(s - m_new)
    l_ref[...] = alpha * l_ref[...] + jnp.sum(p, axis=-1, keepdims=True)
    acc_ref[...] = alpha * acc_ref[...] + jnp.dot(p.astype(BF16), v, preferred_element_type=F32)
    m_ref[...] = m_new


def _init_softmax(m_ref, l_ref, acc_ref):
    m_ref[...] = jnp.full(m_ref.shape, NEG, F32)
    l_ref[...] = jnp.zeros(l_ref.shape, F32)
    acc_ref[...] = jnp.zeros(acc_ref.shape, F32)


def _causal_mask(tq):
    r = lax.broadcasted_iota(I32, (tq, tq), 0)
    c = lax.broadcasted_iota(I32, (tq, tq), 1)
    return c <= r


def _moba_kernel(q_ref, c128_ref, s128_ref, kmean_ref, k_ref, v_ref, o_ref,
                 q16_ref, bits_ref, m_ref, l_ref, acc_ref):
    i = pl.program_id(2)
    j = pl.program_id(3)
    tq = ATT_TILE
    per_tile = tq // MOBA_BLOCK
    nb = kmean_ref.shape[2]

    @pl.when(j == 0)
    def _():
        qf = _rope128(q_ref[...], c128_ref[...], s128_ref[...])
        q16_ref[...] = (qf * (MOBA_HEAD_DIM ** -0.5)).astype(BF16)
        gate_t = lax.dot_general(kmean_ref[0, 0], qf, (((1,), (1,)), ((), ())),
                                 precision=HI, preferred_element_type=F32)
        blk = lax.broadcasted_iota(I32, (nb, tq), 0)
        own = per_tile * i + lax.broadcasted_iota(I32, (nb, tq), 1) // MOBA_BLOCK
        past = blk < own
        geff = jnp.where(past, gate_t, -jnp.inf)
        rank = jnp.zeros((nb, tq), F32)
        for jj in range(nb):
            row = geff[jj:jj + 1, :]
            beats = jnp.where(row > geff, 1.0, jnp.where(row == geff, jnp.where(blk > jj, 1.0, 0.0), 0.0))
            rank = rank + beats
        sel = jnp.where(past, jnp.where(rank < float(MOBA_TOPK), 1.0, 0.0), 0.0)
        weight = jnp.left_shift(jnp.ones((nb, tq), I32), blk).astype(F32)
        selw = (sel * weight).astype(BF16)
        bits = lax.dot_general(selw, jnp.ones((nb, LANES), BF16), (((0,), (0,)), ((), ())),
                               preferred_element_type=F32)
        bits_ref[...] = bits.astype(I32)
        _init_softmax(m_ref, l_ref, acc_ref)

    @pl.when(j <= i)
    def _():
        kv = i - j
        s = lax.dot_general(q16_ref[...], k_ref[...], (((1,), (1,)), ((), ())), preferred_element_type=F32)
        bits = jnp.concatenate([bits_ref[...]] * (tq // LANES), axis=1)
        kblk = per_tile * kv + lax.broadcasted_iota(I32, (tq, tq), 1) // MOBA_BLOCK
        chosen = (lax.shift_right_logical(bits, kblk) & 1) == 1

        def finish(allowed):
            _online_step(jnp.where(allowed, s, NEG), v_ref[...], m_ref, l_ref, acc_ref)

        @pl.when(j == 0)
        def _():
            r = lax.broadcasted_iota(I32, (tq, tq), 0)
            c = lax.broadcasted_iota(I32, (tq, tq), 1)
            own_blk = (r // MOBA_BLOCK == c // MOBA_BLOCK) & (c <= r)
            finish(chosen | own_blk)

        @pl.when(j > 0)
        def _():
            finish(chosen)

    @pl.when(j == i)
    def _():
        o_ref[...] = (acc_ref[...] / l_ref[...]).astype(BF16)


def _moba_branch(u_main, mk16, kmean, mv16, batch, seq, tabs):
    tq = ATT_TILE
    nq = seq // tq
    nb = seq // MOBA_BLOCK
    qblk = COL_MQ // LANES

    def kvmap(b, h, i, j):
        return (b * nq + jnp.maximum(i - j, 0), h)

    return pl.pallas_call(
        _moba_kernel,
        grid=(batch, MOBA_HEADS, nq, nq),
        in_specs=[pl.BlockSpec((tq, LANES), lambda b, h, i, j: (b * nq + i, qblk + h)),
                  pl.BlockSpec((tq, LANES), lambda b, h, i, j: (i, 0)),
                  pl.BlockSpec((tq, LANES), lambda b, h, i, j: (i, 0)),
                  pl.BlockSpec((1, 1, nb, LANES), lambda b, h, i, j: (b, h, 0, 0)),
                  pl.BlockSpec((tq, LANES), kvmap),
                  pl.BlockSpec((tq, LANES), kvmap)],
        out_specs=pl.BlockSpec((tq, LANES), lambda b, h, i, j: (b * nq + i, h)),
        out_shape=jax.ShapeDtypeStruct((batch * seq, BRANCH_WIDTH), BF16),
        scratch_shapes=[pltpu.VMEM((tq, LANES), BF16), pltpu.VMEM((tq, LANES), I32),
                        pltpu.VMEM((tq, 1), F32), pltpu.VMEM((tq, 1), F32), pltpu.VMEM((tq, LANES), F32)],
        compiler_params=_cparams(("parallel", "parallel", "parallel", "arbitrary"), 32),
        name="moba_attention",
    )(u_main, tabs["c128"], tabs["s128"], kmean, mk16, mv16)


def _mla_kernel(qn_ref, qp_ref, c64_ref, sa_ref, sb_ref, kn_ref, kp_ref, v_ref, o_ref,
                q16_ref, m_ref, l_ref, acc_ref):
    h = pl.program_id(1)
    i = pl.program_id(2)
    j = pl.program_id(3)
    tq = ATT_TILE
    scale = (MLA_NOPE_DIM + MLA_ROPE_DIM) ** -0.5

    @pl.when(j == 0)
    def _():
        q16_ref[:, :LANES] = (qn_ref[...] * scale).astype(BF16)
        qp = _rope64x2(qp_ref[...], c64_ref[...], sa_ref[...], sb_ref[...]) * scale
        lane = lax.broadcasted_iota(I32, qp.shape, 1)
        q16_ref[:, LANES:] = jnp.where(lane // MLA_ROPE_DIM == h % 2, qp, 0.0).astype(BF16)
        _init_softmax(m_ref, l_ref, acc_ref)

    @pl.when(j <= i)
    def _():
        kcat = jnp.concatenate([kn_ref[...], kp_ref[...]], axis=1)
        s = lax.dot_general(q16_ref[...], kcat, (((1,), (1,)), ((), ())), preferred_element_type=F32)

        @pl.when(j == i)
        def _():
            _online_step(jnp.where(_causal_mask(tq), s, NEG), v_ref[...], m_ref, l_ref, acc_ref)

        @pl.when(j < i)
        def _():
            _online_step(s, v_ref[...], m_ref, l_ref, acc_ref)

    @pl.when(j == i)
    def _():
        o_ref[...] = (acc_ref[...] / l_ref[...]).astype(BF16)


def _mla_branch(q_up, kv_up, kpe, batch, seq, tabs):
    tq = ATT_TILE
    nq = seq // tq

    def kvmap(off):
        return lambda b, h, i, j: (b * nq + jnp.minimum(j, i), off + h)

    def tab():
        return pl.BlockSpec((tq, LANES), lambda b, h, i, j: (i, 0))

    return pl.pallas_call(
        _mla_kernel,
        grid=(batch, MLA_HEADS, nq, nq),
        in_specs=[pl.BlockSpec((tq, LANES), lambda b, h, i, j: (b * nq + i, h)),
                  pl.BlockSpec((tq, LANES), lambda b, h, i, j: (b * nq + i, MLA_HEADS + h // 2)),
                  tab(), tab(), tab(),
                  pl.BlockSpec((tq, LANES), kvmap(0)),
                  pl.BlockSpec((tq, LANES), lambda b, h, i, j: (b * nq + jnp.minimum(j, i), 0)),
                  pl.BlockSpec((tq, LANES), kvmap(MLA_HEADS))],
        out_specs=pl.BlockSpec((tq, LANES), lambda b, h, i, j: (b * nq + i, h)),
        out_shape=jax.ShapeDtypeStruct((batch * seq, BRANCH_WIDTH), BF16),
        scratch_shapes=[pltpu.VMEM((tq, 2 * LANES), BF16),
                        pltpu.VMEM((tq, 1), F32), pltpu.VMEM((tq, 1), F32), pltpu.VMEM((tq, LANES), F32)],
        compiler_params=_cparams(("parallel", "parallel", "parallel", "arbitrary"), 32),
        name="mla_attention",
    )(q_up, q_up, tabs["c64"], tabs["sa"], tabs["sb"], kv_up, kpe, kv_up)


def _diff_kernel(lam_init, q_ref, c64_ref, sa_ref, sb_ref, k_ref, v_ref, lq1_ref, lk1_ref, lq2_ref, lk2_ref,
                 g_ref, o_ref, q1_ref, q2_ref, m1_ref, l1_ref, a1_ref, m2_ref, l2_ref, a2_ref):
    i = pl.program_id(2)
    j = pl.program_id(3)
    tq = ATT_TILE

    @pl.when(j == 0)
    def _():
        q = _rope64x2(q_ref[...], c64_ref[...], sa_ref[...], sb_ref[...]) * (DIFF_HEAD_DIM ** -0.5)
        lane = lax.broadcasted_iota(I32, q.shape, 1)
        q1_ref[...] = jnp.where(lane < DIFF_HEAD_DIM, q, 0.0).astype(BF16)
        q2_ref[...] = jnp.where(lane < DIFF_HEAD_DIM, 0.0, q).astype(BF16)
        _init_softmax(m1_ref, l1_ref, a1_ref)
        _init_softmax(m2_ref, l2_ref, a2_ref)

    @pl.when(j <= i)
    def _():
        k = k_ref[...]
        nt = (((1,), (1,)), ((), ()))
        s1 = lax.dot_general(q1_ref[...], k, nt, preferred_element_type=F32)
        s2 = lax.dot_general(q2_ref[...], k, nt, preferred_element_type=F32)

        @pl.when(j == i)
        def _():
            mask = _causal_mask(tq)
            _online_step(jnp.where(mask, s1, NEG), v_ref[...], m1_ref, l1_ref, a1_ref)
            _online_step(jnp.where(mask, s2, NEG), v_ref[...], m2_ref, l2_ref, a2_ref)

        @pl.when(j < i)
        def _():
            _online_step(s1, v_ref[...], m1_ref, l1_ref, a1_ref)
            _online_step(s2, v_ref[...], m2_ref, l2_ref, a2_ref)

    @pl.when(j == i)
    def _():
        lam = (jnp.exp(jnp.sum(lq1_ref[...] * lk1_ref[...], axis=-1, keepdims=True))
               - jnp.exp(jnp.sum(lq2_ref[...] * lk2_ref[...], axis=-1, keepdims=True)) + lam_init)
        o = a1_ref[...] / l1_ref[...] - lam * (a2_ref[...] / l2_ref[...])
        o_ref[...] = (_rms_rows(o, g_ref[...]) * (1.0 - lam_init)).astype(BF16)


def _diff_branch(u_main, dk16, dv16, batch, seq, tabs, lq1, lk1, lq2, lk2, subln_g, lam_init):
    tq = ATT_TILE
    nq = seq // tq
    qblk = COL_DQ // LANES

    def kvmap(b, h, i, j):
        return (b * nq + jnp.minimum(j, i), h)

    def tab():
        return pl.BlockSpec((tq, LANES), lambda b, h, i, j: (i, 0))

    def vec(n):
        return pl.BlockSpec((1, n), lambda b, h, i, j: (0, 0))

    stat = [pltpu.VMEM((tq, 1), F32), pltpu.VMEM((tq, 1), F32), pltpu.VMEM((tq, LANES), F32)]
    return pl.pallas_call(
        functools.partial(_diff_kernel, lam_init),
        grid=(batch, DIFF_HEADS, nq, nq),
        in_specs=[pl.BlockSpec((tq, LANES), lambda b, h, i, j: (b * nq + i, qblk + h)),
                  tab(), tab(), tab(),
                  pl.BlockSpec((tq, LANES), kvmap), pl.BlockSpec((tq, LANES), kvmap),
                  vec(DIFF_HEAD_DIM), vec(DIFF_HEAD_DIM), vec(DIFF_HEAD_DIM), vec(DIFF_HEAD_DIM),
                  vec(DIFF_V_DIM)],
        out_specs=pl.BlockSpec((tq, LANES), lambda b, h, i, j: (b * nq + i, h)),
        out_shape=jax.ShapeDtypeStruct((batch * seq, BRANCH_WIDTH), BF16),
        scratch_shapes=[pltpu.VMEM((tq, LANES), BF16), pltpu.VMEM((tq, LANES), BF16)] + stat + stat,
        compiler_params=_cparams(("parallel", "parallel", "parallel", "arbitrary"), 32),
        name="diff_attention",
    )(u_main, tabs["c64"], tabs["sa"], tabs["sb"], dk16, dv16,
      lq1.reshape(1, -1), lk1.reshape(1, -1), lq2.reshape(1, -1), lk2.reshape(1, -1), subln_g.reshape(1, -1))


def _merge_kernel(h_ref, y_ref, wg_ref, bg_ref, wp_ref, o_ref, acc_ref):
    b = pl.program_id(2)
    gate = _sigmoid(jnp.dot(h_ref[...], wg_ref[0], preferred_element_type=F32) + bg_ref[0])
    term = gate * jnp.dot(y_ref[0], wp_ref[0], preferred_element_type=F32)

    @pl.when(b == 0)
    def _():
        acc_ref[...] = term

    @pl.when(b > 0)
    def _():
        acc_ref[...] += term

    @pl.when(b == N_BRANCHES - 1)
    def _():
        o_ref[...] = acc_ref[...].astype(BF16)


def _gated_merge(h16, ys, w_gate16, b_gate, w_branch16, tm=1024, tn=512):
    t, d = h16.shape
    tm = min(tm, t)
    return pl.pallas_call(
        _merge_kernel,
        grid=(t // tm, d // tn, N_BRANCHES),
        in_specs=[pl.BlockSpec((tm, d), lambda i, j, b: (i, 0)),
                  pl.BlockSpec((1, tm, BRANCH_WIDTH), lambda i, j, b: (b, i, 0)),
                  pl.BlockSpec((1, d, tn), lambda i, j, b: (b, 0, j)),
                  pl.BlockSpec((1, 1, tn), lambda i, j, b: (b, 0, j)),
                  pl.BlockSpec((1, BRANCH_WIDTH, tn), lambda i, j, b: (b, 0, j))],
        out_specs=pl.BlockSpec((tm, tn), lambda i, j, b: (i, j)),
        out_shape=jax.ShapeDtypeStruct((t, d), BF16),
        scratch_shapes=[pltpu.VMEM((tm, tn), F32)],
        compiler_params=_cparams(("parallel", "parallel", "arbitrary"), 48),
        name="gated_merge",
    )(h16, ys, w_gate16, b_gate.reshape(N_BRANCHES, 1, d), w_branch16)


def _router_kernel(h_ref, w_ref, o_ref):
    logits = jnp.dot(h_ref[...], w_ref[...], precision=HI, preferred_element_type=F32)
    lane = lax.broadcasted_iota(I32, logits.shape, 1)
    lanef = lane.astype(F32)
    big = float(LANES)
    is_g = (lane >= MOE_EXPERTS) & (lane < MOE_EXPERTS + MOE_GROUPS)
    gl = jnp.where(is_g, logits, -jnp.inf)
    gmax = jnp.max(gl, axis=-1, keepdims=True)
    gsel = jnp.min(jnp.where(gl == gmax, lanef, big), axis=-1, keepdims=True) - MOE_EXPERTS
    g_w = 1.0 / jnp.sum(jnp.exp(gl - gmax), axis=-1, keepdims=True)
    in_grp = (lane < MOE_EXPERTS) & ((lane // MOE_EXPERTS_PER_GROUP).astype(F32) == gsel)
    el = jnp.where(in_grp, logits, -jnp.inf)
    m1 = jnp.max(el, axis=-1, keepdims=True)
    i1 = jnp.min(jnp.where(el == m1, lanef, big), axis=-1, keepdims=True)
    el2 = jnp.where(lanef == i1, -jnp.inf, el)
    m2 = jnp.max(el2, axis=-1, keepdims=True)
    i2 = jnp.min(jnp.where(el2 == m2, lanef, big), axis=-1, keepdims=True)
    e2 = jnp.exp(m2 - m1)
    w1 = g_w / (1.0 + e2)
    w2 = g_w * e2 / (1.0 + e2)
    o_ref[...] = jnp.where(lane == 0, i1, jnp.where(lane == 1, i2, jnp.where(lane == 2, w1,
                           jnp.where(lane == 3, w2, 0.0))))


def _router(h32, w_router):
    t, d = h32.shape
    return pl.pallas_call(
        _router_kernel,
        grid=(t // ROW_TILE,),
        in_specs=[pl.BlockSpec((ROW_TILE, d), lambda i: (i, 0)), pl.BlockSpec((d, LANES), lambda i: (0, 0))],
        out_specs=pl.BlockSpec((ROW_TILE, LANES), lambda i: (i, 0)),
        out_shape=jax.ShapeDtypeStruct((t, LANES), F32),
        compiler_params=_cparams(("parallel",), 32),
        name="moe_router",
    )(h32, w_router)


def _row_copy(src_hbm, row, dst, dst_row, sem):
    return pltpu.make_async_copy(src_hbm.at[pl.ds(row, 1)], dst.at[pl.ds(dst_row, 1)], sem)


def _gmm_kernel(tile_expert_ref, ntiles_ref, src_ref, h_hbm, wg_ref, wu_ref, wd_ref, o_ref, xbuf, sems):
    i = pl.program_id(0)
    ntiles = ntiles_ref[0]
    tm = MOE_TILE

    def gather(tile, slot, start):
        def body(r, carry):
            cp = _row_copy(h_hbm, src_ref[tile * tm + r], xbuf.at[slot], r, sems.at[slot])
            if start:
                cp.start()
            else:
                cp.wait()
            return carry
        lax.fori_loop(0, tm, body, 0)

    @pl.when(i == 0)
    def _():
        gather(0, 0, True)

    @pl.when(i + 1 < ntiles)
    def _():
        gather(i + 1, (i + 1) % 2, True)

    @pl.when(i < ntiles)
    def _():
        slot = i % 2
        gather(i, slot, False)
        x = xbuf[slot].astype(BF16)
        a = jnp.dot(x, wg_ref[0], preferred_element_type=F32)
        u = jnp.dot(x, wu_ref[0], preferred_element_type=F32)
        hid = (_silu(a) * u).astype(BF16)
        o_ref[...] = jnp.dot(hid, wd_ref[0], preferred_element_type=F32)

    @pl.when(i >= ntiles)
    def _():
        o_ref[...] = jnp.zeros(o_ref.shape, F32)


def _grouped_experts(h32, src_tok, tile_expert, ntiles, wg16, wu16, wd16, max_tiles):
    t, d = h32.shape
    tm = MOE_TILE

    def wmap(i, te, nt, src):
        return (te[jnp.minimum(i, nt[0] - 1)], 0, 0)

    grid_spec = pltpu.PrefetchScalarGridSpec(
        num_scalar_prefetch=3,
        grid=(max_tiles,),
        in_specs=[pl.BlockSpec(memory_space=pl.ANY),
                  pl.BlockSpec((1, d, MOE_D_FF), wmap),
                  pl.BlockSpec((1, d, MOE_D_FF), wmap),
                  pl.BlockSpec((1, MOE_D_FF, d), wmap)],
        out_specs=pl.BlockSpec((tm, d), lambda i, te, nt, src: (i, 0)),
        scratch_shapes=[pltpu.VMEM((2, tm, d), F32), pltpu.SemaphoreType.DMA((2,))],
    )
    return pl.pallas_call(
        _gmm_kernel,
        grid_spec=grid_spec,
        out_shape=jax.ShapeDtypeStruct((max_tiles * tm, d), F32),
        compiler_params=_cparams(("arbitrary",), 48),
        name="moe_grouped_experts",
    )(tile_expert, ntiles, src_tok, h32, wg16, wu16, wd16)


def _combine_kernel(dest_ref, h_ref, rw_ref, y_hbm, g_ref, b_ref, o32_ref, o16_ref, ybuf, sems):
    i = pl.program_id(0)
    n = pl.num_programs(0)
    tm = ROW_TILE

    def gather(tile, slot, start):
        def body(r, carry):
            for k in range(MOE_TOPK):
                cp = _row_copy(y_hbm, dest_ref[(tile * tm + r) * MOE_TOPK + k], ybuf.at[slot, k], r,
                               sems.at[slot])
                if start:
                    cp.start()
                else:
                    cp.wait()
            return carry
        lax.fori_loop(0, tm, body, 0)

    @pl.when(i == 0)
    def _():
        gather(0, 0, True)

    @pl.when(i + 1 < n)
    def _():
        gather(i + 1, (i + 1) % 2, True)

    slot = i % 2
    gather(i, slot, False)
    rw = rw_ref[...]
    ffn = rw[:, 2:3] * ybuf[slot, 0] + rw[:, 3:4] * ybuf[slot, 1]
    y = _ln_rows(DEEPNORM_ALPHA * h_ref[...] + ffn, g_ref[...], b_ref[...])
    o32_ref[...] = y
    o16_ref[...] = y.astype(BF16)


def _moe_combine_ln(h32, route, dest, y_sorted, g, b):
    t, d = h32.shape
    tm = ROW_TILE
    row = pl.BlockSpec((tm, d), lambda i, dst: (i, 0))
    vec = pl.BlockSpec((1, d), lambda i, dst: (0, 0))
    grid_spec = pltpu.PrefetchScalarGridSpec(
        num_scalar_prefetch=1,
        grid=(t // tm,),
        in_specs=[row, pl.BlockSpec((tm, LANES), lambda i, dst: (i, 0)),
                  pl.BlockSpec(memory_space=pl.ANY), vec, vec],
        out_specs=[row, row],
        scratch_shapes=[pltpu.VMEM((2, MOE_TOPK, tm, d), F32), pltpu.SemaphoreType.DMA((2,))],
    )
    return pl.pallas_call(
        _combine_kernel,
        grid_spec=grid_spec,
        out_shape=[jax.ShapeDtypeStruct((t, d), F32), jax.ShapeDtypeStruct((t, d), BF16)],
        compiler_params=_cparams(("arbitrary",), 56),
        name="moe_combine_layer_norm",
    )(dest, h32, route, y_sorted, g.reshape(1, d), b.reshape(1, d))


def _moe_plan(route, max_tiles):
    t = route.shape[0]
    tm = MOE_TILE
    expert = route[:, :MOE_TOPK].astype(I32).reshape(-1)
    onehot = (expert[:, None] == jnp.arange(MOE_EXPERTS, dtype=I32)[None, :]).astype(I32)
    rank = jnp.sum((jnp.cumsum(onehot, axis=0) - onehot) * onehot, axis=1)
    counts = jnp.sum(onehot, axis=0)
    tiles_per = (counts + tm - 1) // tm
    tile_end = jnp.cumsum(tiles_per)
    tile_start = tile_end - tiles_per
    dest = tile_start[expert] * tm + rank
    ntiles = tile_end[-1:].astype(I32)
    tile_expert = jnp.minimum(
        jnp.searchsorted(tile_end, jnp.arange(max_tiles, dtype=I32), side="right"), MOE_EXPERTS - 1).astype(I32)
    src_tok = jnp.zeros((max_tiles * tm,), I32).at[dest].set(jnp.arange(MOE_TOPK * t, dtype=I32) // MOE_TOPK)
    return dest.astype(I32), src_tok, tile_expert, ntiles


def _hier_moe_ln(h32, w_router, wg16, wu16, wd16, ln_g, ln_b):
    t = h32.shape[0]
    max_tiles = (MOE_TOPK * t) // MOE_TILE + MOE_EXPERTS
    route = _router(h32, w_router)
    dest, src_tok, tile_expert, ntiles = _moe_plan(route, max_tiles)
    y_sorted = _grouped_experts(h32, src_tok, tile_expert, ntiles, wg16, wu16, wd16, max_tiles)
    return _moe_combine_ln(h32, route, dest, y_sorted, ln_g, ln_b)


def _rope_tables(seq):
    def base(dim):
        inv = 1.0 / (ROPE_THETA ** (jnp.arange(0, dim, 2, dtype=F32) / dim))
        ang = jnp.arange(seq, dtype=F32)[:, None] * inv[None, :]
        return jnp.cos(ang), jnp.sin(ang)

    c, s = base(MOBA_HEAD_DIM)
    c6, s6 = base(DIFF_HEAD_DIM)
    z = jnp.zeros_like(s6)
    return {"c128": jnp.concatenate([c, c], axis=1), "s128": jnp.concatenate([-s, s], axis=1),
            "c64": jnp.concatenate([c6] * 4, axis=1),
            "sa": jnp.concatenate([-s6, z, -s6, z], axis=1),
            "sb": jnp.concatenate([z, s6, z, s6], axis=1)}


def _split_w_in(w):
    dt0 = SSM_WIDTH + SSM_CONV_DIM
    mq0 = dt0 + SSM_HEADS
    kr0 = mq0 + 3 * BRANCH_WIDTH + MLA_Q_RANK + MLA_KV_RANK
    dq0 = kr0 + MLA_ROPE_DIM
    main = jnp.concatenate([w[:, :dt0], w[:, mq0:kr0], w[:, dq0:]], axis=1).astype(BF16)
    pad = jnp.zeros((w.shape[0], LANES - MLA_ROPE_DIM - SSM_HEADS), w.dtype)
    tail = jnp.concatenate([w[:, kr0:dq0], w[:, dt0:mq0], pad], axis=1).astype(BF16)
    return main, tail


def _permute_mla(w_uq, w_ukv):
    hq = MLA_NOPE_DIM + MLA_ROPE_DIM
    uq = w_uq.reshape(MLA_Q_RANK, MLA_HEADS, hq)
    uq = jnp.concatenate([uq[:, :, :MLA_NOPE_DIM].reshape(MLA_Q_RANK, -1),
                          uq[:, :, MLA_NOPE_DIM:].reshape(MLA_Q_RANK, -1)], axis=1).astype(BF16)
    ukv = w_ukv.reshape(MLA_KV_RANK, MLA_HEADS, MLA_NOPE_DIM + MLA_V_DIM)
    ukv = jnp.concatenate([ukv[:, :, :MLA_NOPE_DIM].reshape(MLA_KV_RANK, -1),
                           ukv[:, :, MLA_NOPE_DIM:].reshape(MLA_KV_RANK, -1)], axis=1).astype(BF16)
    return uq, ukv


def kernel(x, emb_ln_g, emb_ln_b, w_in, conv_w, conv_b, dt_bias, a_log, d_skip, ssm_norm_g, mla_q_norm_g, mla_kv_norm_g, mla_w_uq, mla_w_ukv, diff_lambda_q1, diff_lambda_k1, diff_lambda_q2, diff_lambda_k2, diff_subln_g, w_gate, b_gate, w_branch, w_out, ln1_g, ln1_b, router_group, router_expert, moe_w_gate, moe_w_up, moe_w_down, ln2_g, ln2_b):
    batch, seq, d = x.shape
    t = batch * seq
    depth = w_in.shape[0]
    tabs = _rope_tables(seq)
    h32, h16 = _layer_norm(x.reshape(t, d), emb_ln_g, emb_ln_b)
    for l in range(depth):
        w_main, w_tail = _split_w_in(w_in[l])
        u_main = _matmul(h16, w_main, F32, 512, 1024, "in_proj")
        u_tail = _matmul(h16, w_tail, F32, 512, LANES, "in_proj_tail")
        y_ssd = _ssd_branch(u_main, u_tail, batch, seq, conv_w[l], conv_b[l], dt_bias[l], a_log[l],
                            d_skip[l], ssm_norm_g[l])
        mk16, kmean, mv16, cqn, ckvn, kpe, dk16, dv16 = _kv_prep(
            u_main, u_tail, seq, tabs, mla_q_norm_g[l], mla_kv_norm_g[l])
        nb = seq // MOBA_BLOCK
        kmean = kmean.reshape(batch, nb, MOBA_HEADS, LANES).transpose(0, 2, 1, 3)
        y_moba = _moba_branch(u_main, mk16, kmean, mv16, batch, seq, tabs)
        w_uq, w_ukv = _permute_mla(mla_w_uq[l], mla_w_ukv[l])
        q_up = _matmul(cqn, w_uq, F32, 1024, 512, "mla_q_up")
        kv_up = _matmul(ckvn, w_ukv, BF16, 1024, 1024, "mla_kv_up")
        y_mla = _mla_branch(q_up, kv_up, kpe, batch, seq, tabs)
        lam_init = 0.8 - 0.6 * math.exp(-0.3 * l)
        y_diff = _diff_branch(u_main, dk16, dv16, batch, seq, tabs, diff_lambda_q1[l], diff_lambda_k1[l],
                              diff_lambda_q2[l], diff_lambda_k2[l], diff_subln_g[l], lam_init)
        ys = jnp.stack([y_ssd, y_moba, y_mla, y_diff])
        merged = _gated_merge(h16, ys, w_gate[l].astype(BF16), b_gate[l], w_branch[l].astype(BF16))
        mix = _matmul(merged, w_out[l].astype(BF16), F32, 1024, 1024, "out_proj")
        h32, h16 = _layer_norm(mix, ln1_g[l], ln1_b[l], residual=h32)
        w_router = jnp.concatenate(
            [router_expert[l], router_group[l],
             jnp.zeros((d, LANES - MOE_EXPERTS - MOE_GROUPS), F32)], axis=1)
        h32, h16 = _hier_moe_ln(
            h32, w_router,
            moe_w_gate[l].reshape(MOE_EXPERTS, d, MOE_D_FF).astype(BF16),
            moe_w_up[l].reshape(MOE_EXPERTS, d, MOE_D_FF).astype(BF16),
            moe_w_down[l].reshape(MOE_EXPERTS, MOE_D_FF, d).astype(BF16),
            ln2_g[l], ln2_b[l])
    return h32.reshape(batch, seq, d)
```

```python
import functools
import math

import jax
import jax.numpy as jnp
import numpy as np
from jax import lax
from jax.experimental import pallas as pl
from jax.experimental.pallas import tpu as pltpu

F32 = jnp.float32
BF16 = jnp.bfloat16
I32 = jnp.int32
HI = lax.Precision.HIGHEST

LANES = 128
SUBLANES = 8
MIB = 1024 * 1024

D_MODEL = 4096
DEPTH = 2
BRANCH_WIDTH = 1024
N_BRANCHES = 4
ROPE_THETA = 10000.0
LN_EPS = 1e-5
RMS_EPS = 1e-6
DEEPNORM_ALPHA = (2 * DEPTH) ** 0.25

SSM_HEADS = 16
SSM_HEAD_DIM = 64
SSM_WIDTH = SSM_HEADS * SSM_HEAD_DIM
SSM_GROUPS = 4
SSM_STATE = 128
SSM_CONV = 4
SSM_CHUNK = 128
SSM_CONV_DIM = SSM_WIDTH + 2 * SSM_GROUPS * SSM_STATE

MOBA_HEADS = 8
MOBA_HEAD_DIM = 128
MOBA_BLOCK = 256
MOBA_TOPK = 3

MLA_HEADS = 8
MLA_Q_RANK = 768
MLA_KV_RANK = 256
MLA_NOPE_DIM = 128
MLA_ROPE_DIM = 64
MLA_V_DIM = 128

DIFF_HEADS = 8
DIFF_HEAD_DIM = 64
DIFF_V_DIM = 2 * DIFF_HEAD_DIM

MOE_GROUPS = 4
MOE_EXPERTS_PER_GROUP = 8
MOE_EXPERTS = MOE_GROUPS * MOE_EXPERTS_PER_GROUP
MOE_TOPK = 2
MOE_D_FF = 512

U_MAIN = 10240
COL_Z, COL_X, COL_BC = 0, 1024, 2048
COL_MQ, COL_MK, COL_MV = 3072, 4096, 5120
COL_CQ, COL_CKV = 6144, 6912
COL_DQ, COL_DK, COL_DV = 7168, 8192, 9216
TAIL_DT = 64

ATT_TILE = 512
ROW_TILE = 256
MOE_TILE = 256
MOE_FF_SPLIT = 2
NEG = -1e30


def _cparams(sem, vmem_mib):
    return pltpu.CompilerParams(dimension_semantics=sem, vmem_limit_bytes=vmem_mib * MIB)


def _silu(x):
    return x * (1.0 / (1.0 + jnp.exp(-x)))


def _sigmoid(x):
    return 1.0 / (1.0 + jnp.exp(-x))


def _rope128(t, cos_f, sin_s):
    return t * cos_f + pltpu.roll(t, 64, 1) * sin_s


def _rope64x2(t, cos_f, sin_a, sin_b):
    return t * cos_f + pltpu.roll(t, 96, 1) * sin_a + pltpu.roll(t, 32, 1) * sin_b


def _ln_rows(x, g, b):
    mu = jnp.mean(x, axis=-1, keepdims=True)
    xc = x - mu
    var = jnp.mean(xc * xc, axis=-1, keepdims=True)
    return xc * lax.rsqrt(var + LN_EPS) * g + b


def _ln_kernel(x_ref, g_ref, b_ref, o32_ref, o16_ref):
    y = _ln_rows(x_ref[...], g_ref[...], b_ref[...])
    o32_ref[...] = y
    o16_ref[...] = y.astype(BF16)


def _res_ln_kernel(h_ref, m_ref, g_ref, b_ref, o32_ref, o16_ref):
    y = _ln_rows(DEEPNORM_ALPHA * h_ref[...] + m_ref[...], g_ref[...], b_ref[...])
    o32_ref[...] = y
    o16_ref[...] = y.astype(BF16)


def _layer_norm(x, g, b, residual=None):
    t, d = x.shape
    row = pl.BlockSpec((ROW_TILE, d), lambda i: (i, 0))
    vec = pl.BlockSpec((1, d), lambda i: (0, 0))
    ins = [x] if residual is None else [residual, x]
    return pl.pallas_call(
        _ln_kernel if residual is None else _res_ln_kernel,
        grid=(t // ROW_TILE,),
        in_specs=[row] * len(ins) + [vec, vec],
        out_specs=[row, row],
        out_shape=[jax.ShapeDtypeStruct((t, d), F32), jax.ShapeDtypeStruct((t, d), BF16)],
        compiler_params=_cparams(("parallel",), 48),
        name="layer_norm" if residual is None else "residual_layer_norm",
    )(*ins, g.reshape(1, d), b.reshape(1, d))


def _mm_kernel(a_ref, b_ref, o_ref):
    o_ref[...] = jnp.dot(a_ref[...], b_ref[0], preferred_element_type=F32).astype(o_ref.dtype)


def _matmul(a, b, layer, out_dtype, tm, tn, name):
    m, k = a.shape
    n = b.shape[2]
    tm, tn = min(tm, m), min(tn, n)
    assert m % tm == 0 and n % tn == 0
    return pl.pallas_call(
        _mm_kernel,
        grid=(n // tn, m // tm),
        in_specs=[pl.BlockSpec((tm, k), lambda j, i: (i, 0)),
                  pl.BlockSpec((1, k, tn), lambda j, i: (layer, 0, j))],
        out_specs=pl.BlockSpec((tm, tn), lambda j, i: (i, j)),
        out_shape=jax.ShapeDtypeStruct((m, n), out_dtype),
        compiler_params=_cparams(("parallel", "parallel"), 48),
        name=name,
    )(a, b)


def _ssd_kernel(z_ref, x_ref, bc_ref, tail_ref, cw_ref, cb_ref, dtb_ref, aneg_ref, dsk_ref, ng_ref,
                o_ref, extx_ref, extbc_ref, state_ref):
    c = pl.program_id(1)
    L = SSM_CHUNK

    @pl.when(c == 0)
    def _():
        extx_ref[0:SUBLANES, :] = jnp.zeros((SUBLANES, SSM_WIDTH), F32)
        extbc_ref[0:SUBLANES, :] = jnp.zeros((SUBLANES, SSM_WIDTH), F32)
        state_ref[...] = jnp.zeros(state_ref.shape, F32)

    def conv_silu(cur_ref, ext_ref, w, b):
        ext_ref[SUBLANES:SUBLANES + L, :] = cur_ref[...]
        acc = b
        for k in range(SSM_CONV):
            off = SUBLANES - (SSM_CONV - 1) + k
            acc = acc + w[k:k + 1, :] * ext_ref[off:off + L, :]
        ext_ref[0:SUBLANES, :] = ext_ref[L:L + SUBLANES, :]
        return _silu(acc)

    cw = cw_ref[...]
    cb = cb_ref[...]
    xs = conv_silu(x_ref, extx_ref, cw[:, :SSM_WIDTH], cb[:, :SSM_WIDTH])
    bc = conv_silu(bc_ref, extbc_ref, cw[:, SSM_WIDTH:], cb[:, SSM_WIDTH:])
    gn = SSM_GROUPS * SSM_STATE

    dtr = tail_ref[...] + dtb_ref[...]
    dt = jnp.maximum(dtr, 0.0) + jnp.log1p(jnp.exp(-jnp.abs(dtr)))
    a = dt * aneg_ref[...]
    ri = lax.broadcasted_iota(I32, (L, L), 0)
    ci = lax.broadcasted_iota(I32, (L, L), 1)
    tril = ri >= ci
    tril_f = tril.astype(F32)
    a_cs = jnp.dot(tril_f, a, precision=HI, preferred_element_type=F32)
    a_cs_t = lax.dot_general(a, (ci >= ri).astype(F32), (((0,), (0,)), ((), ())),
                             precision=HI, preferred_element_type=F32)
    er = lax.broadcasted_iota(I32, (LANES, SSM_WIDTH), 0)
    ec = lax.broadcasted_iota(I32, (LANES, SSM_WIDTH), 1)
    expand = (er - TAIL_DT == ec // SSM_HEAD_DIM).astype(F32)
    dt_e = jnp.dot(dt, expand, precision=HI, preferred_element_type=F32)
    acs_e = jnp.dot(a_cs, expand, precision=HI, preferred_element_type=F32)
    tot_e = acs_e[L - 1:L, :]
    ecs_e = jnp.exp(acs_e)
    etot_e = jnp.exp(tot_e)
    xdt = xs * dt_e
    xd = xdt * jnp.exp(tot_e - acs_e)

    lane = lax.broadcasted_iota(I32, (L, LANES), 1)
    left = lane < SSM_HEAD_DIM
    ys = []
    for g in range(SSM_GROUPS):
        bg = bc[:, g * SSM_STATE:(g + 1) * SSM_STATE].astype(BF16)
        cg = bc[:, gn + g * SSM_STATE:gn + (g + 1) * SSM_STATE].astype(BF16)
        gmat = lax.dot_general(cg, bg, (((1,), (1,)), ((), ())), preferred_element_type=F32)
        for pp in range(2):
            p = 2 * g + pp
            sl = slice(p * LANES, (p + 1) * LANES)
            xdt_p = xdt[:, sl].astype(BF16)
            yh = []
            for h in (2 * p, 2 * p + 1):
                col = a_cs[:, TAIL_DT + h:TAIL_DT + h + 1]
                row = a_cs_t[TAIL_DT + h:TAIL_DT + h + 1, :]
                lmat = jnp.exp(jnp.where(tril, col - row, -jnp.inf))
                yh.append(jnp.dot((gmat * lmat).astype(BF16), xdt_p, preferred_element_type=F32))
            y_diag = jnp.where(left, yh[0], yh[1])
            st = state_ref[p]
            y_off = jnp.dot(cg, st.astype(BF16), preferred_element_type=F32) * ecs_e[:, sl]
            new = lax.dot_general(bg, xd[:, sl].astype(BF16), (((0,), (0,)), ((), ())),
                                  preferred_element_type=F32)
            state_ref[p] = etot_e[:, sl] * st + new
            ys.append(y_diag + y_off + dsk_ref[:, sl] * xs[:, sl])
    y = jnp.concatenate(ys, axis=1) * _silu(z_ref[...])
    ms = jnp.mean(y * y, axis=-1, keepdims=True)
    o_ref[...] = (y * lax.rsqrt(ms + RMS_EPS) * ng_ref[...]).astype(BF16)


def _ssd_branch(u_main, u_tail, batch, seq, conv_w, conv_b, dt_bias, a_log, d_skip, norm_g):
    nc = seq // SSM_CHUNK
    L = SSM_CHUNK

    def lanes16(v):
        return jnp.zeros((1, LANES), F32).at[0, TAIL_DT:TAIL_DT + SSM_HEADS].set(v.astype(F32))

    def col(blk):
        return pl.BlockSpec((L, SSM_WIDTH), lambda b, c: (b * nc + c, blk))

    def vec(n):
        return pl.BlockSpec((1, n), lambda b, c: (0, 0))

    return pl.pallas_call(
        _ssd_kernel,
        grid=(batch, nc),
        in_specs=[col(COL_Z // SSM_WIDTH), col(COL_X // SSM_WIDTH), col(COL_BC // SSM_WIDTH),
                  pl.BlockSpec((L, LANES), lambda b, c: (b * nc + c, 0)),
                  pl.BlockSpec((SSM_CONV, SSM_CONV_DIM), lambda b, c: (0, 0)),
                  vec(SSM_CONV_DIM), vec(LANES), vec(LANES), vec(SSM_WIDTH), vec(SSM_WIDTH)],
        out_specs=pl.BlockSpec((L, SSM_WIDTH), lambda b, c: (b * nc + c, 0)),
        out_shape=jax.ShapeDtypeStruct((batch * seq, SSM_WIDTH), BF16),
        scratch_shapes=[pltpu.VMEM((L + SUBLANES, SSM_WIDTH), F32),
                        pltpu.VMEM((L + SUBLANES, SSM_WIDTH), F32),
                        pltpu.VMEM((SSM_HEADS // 2, SSM_STATE, LANES), F32)],
        compiler_params=_cparams(("parallel", "arbitrary"), 40),
        name="ssd_branch",
    )(u_main, u_main, u_main, u_tail, conv_w, conv_b.reshape(1, -1), lanes16(dt_bias),
      lanes16(-jnp.exp(a_log.astype(F32))), jnp.repeat(d_skip.astype(F32), SSM_HEAD_DIM).reshape(1, -1),
      norm_g.reshape(1, -1))


def _rms_rows(x, g):
    return x * lax.rsqrt(jnp.mean(x * x, axis=-1, keepdims=True) + RMS_EPS) * g


def _kv_prep_kernel(mk_ref, mv_ref, cq_ref, ckv_ref, dk_ref, dv_ref, tail_ref,
                    c128_ref, s128_ref, c64_ref, sa_ref, sb_ref, qg_ref, kvg_ref,
                    mk16_ref, kmean_ref, mv16_ref, cqn_ref, ckvn_ref, kpe_ref, dk16_ref, dv16_ref):
    c128, s128 = c128_ref[...], s128_ref[...]
    c64, sa, sb = c64_ref[...], sa_ref[...], sb_ref[...]
    means = []
    for h in range(MOBA_HEADS):
        sl = slice(h * LANES, (h + 1) * LANES)
        r = _rope128(mk_ref[:, sl], c128, s128)
        mk16_ref[:, sl] = r.astype(BF16)
        means.append(jnp.mean(r, axis=0, keepdims=True))
    kmean_ref[0] = jnp.concatenate(means, axis=0)
    mv16_ref[...] = mv_ref[...].astype(BF16)
    cqn_ref[...] = _rms_rows(cq_ref[...], qg_ref[...]).astype(BF16)
    ckvn_ref[...] = _rms_rows(ckv_ref[...], kvg_ref[...]).astype(BF16)
    r = _rope64x2(tail_ref[...], c64, sa, sb)
    lane = lax.broadcasted_iota(I32, r.shape, 1)
    kpe_ref[...] = jnp.where(lane < MLA_ROPE_DIM, r, pltpu.roll(r, 64, 1)).astype(BF16)
    for h in range(DIFF_HEADS):
        sl = slice(h * LANES, (h + 1) * LANES)
        dk16_ref[:, sl] = _rope64x2(dk_ref[:, sl], c64, sa, sb).astype(BF16)
    dv16_ref[...] = dv_ref[...].astype(BF16)


def _kv_prep(u_main, u_tail, seq, tabs, q_norm_g, kv_norm_g):
    t = u_main.shape[0]
    r = ROW_TILE
    assert r == MOBA_BLOCK
    ns = seq // r

    def col(width, start):
        return pl.BlockSpec((r, width), lambda i: (i, start // width))

    def tab():
        return pl.BlockSpec((r, LANES), lambda i: (i % ns, 0))

    def vec(n):
        return pl.BlockSpec((1, n), lambda i: (0, 0))

    def out(width):
        return pl.BlockSpec((r, width), lambda i: (i, 0))

    return pl.pallas_call(
        _kv_prep_kernel,
        grid=(t // r,),
        in_specs=[col(1024, COL_MK), col(1024, COL_MV), col(MLA_Q_RANK, COL_CQ), col(MLA_KV_RANK, COL_CKV),
                  col(1024, COL_DK), col(1024, COL_DV), pl.BlockSpec((r, LANES), lambda i: (i, 0)),
                  tab(), tab(), tab(), tab(), tab(), vec(MLA_Q_RANK), vec(MLA_KV_RANK)],
        out_specs=[out(1024), pl.BlockSpec((1, MOBA_HEADS, LANES), lambda i: (i, 0, 0)), out(1024),
                   out(MLA_Q_RANK), out(MLA_KV_RANK), out(LANES), out(1024), out(1024)],
        out_shape=[jax.ShapeDtypeStruct((t, 1024), BF16),
                   jax.ShapeDtypeStruct((t // r, MOBA_HEADS, LANES), F32),
                   jax.ShapeDtypeStruct((t, 1024), BF16),
                   jax.ShapeDtypeStruct((t, MLA_Q_RANK), BF16),
                   jax.ShapeDtypeStruct((t, MLA_KV_RANK), BF16),
                   jax.ShapeDtypeStruct((t, LANES), BF16),
                   jax.ShapeDtypeStruct((t, 1024), BF16),
                   jax.ShapeDtypeStruct((t, 1024), BF16)],
        compiler_params=_cparams(("parallel",), 40),
        name="kv_prep",
    )(u_main, u_main, u_main, u_main, u_main, u_main, u_tail,
      tabs["c128"], tabs["s128"], tabs["c64"], tabs["sa"], tabs["sb"],
      q_norm_g.reshape(1, -1), kv_norm_g.reshape(1, -1))


LOG2E = math.log2(math.e)


def _with_ones(v):
    return jnp.concatenate([v, jnp.ones(v.shape, v.dtype)], axis=1)


def _lanes(x, width):
    return jnp.concatenate([x] * (width // LANES), axis=1)


def _online_step(s, v, m_ref, acc_ref):
    m_prev = m_ref[...]
    m_new = jnp.maximum(m_prev, jnp.max(s, axis=-1, keepdims=True))
    alpha = jnp.exp2(m_prev - m_new)
    p = jnp.exp2(s - _lanes(m_new, s.shape[1]))
    acc_ref[...] = _lanes(alpha, acc_ref.shape[1]) * acc_ref[...] + jnp.dot(
        p.astype(BF16), _with_ones(v), preferred_element_type=F32)
    m_ref[...] = m_new


def _first_step(s, v, m_ref, acc_ref):
    m = jnp.broadcast_to(jnp.max(s, axis=-1, keepdims=True), m_ref.shape)
    p = jnp.exp2(s - _lanes(m, s.shape[1]))
    m_ref[...] = m
    acc_ref[...] = jnp.dot(p.astype(BF16), _with_ones(v), preferred_element_type=F32)


def _softmax_out(acc_ref):
    return acc_ref[:, :LANES] / acc_ref[:, LANES:]


_NT = (((1,), (1,)), ((), ()))


def _causal_mask(tq):
    r = lax.broadcasted_iota(I32, (tq, tq), 0)
    c = lax.broadcasted_iota(I32, (tq, tq), 1)
    return c <= r


def _moba_kernel(q_ref, c128_ref, s128_ref, kmean_ref, k_ref, v_ref, o_ref,
                 q16_ref, bits_ref, m_ref, acc_ref):
    i = pl.program_id(2)
    tq = ATT_TILE
    per_tile = tq // MOBA_BLOCK
    nb = kmean_ref.shape[2]

    qf = _rope128(q_ref[...], c128_ref[...], s128_ref[...])
    q16_ref[...] = (qf * (MOBA_HEAD_DIM ** -0.5 * LOG2E)).astype(BF16)
    gate_t = lax.dot_general(kmean_ref[0, 0], qf, _NT, precision=HI, preferred_element_type=F32)
    blk = lax.broadcasted_iota(I32, (nb, tq), 0)
    own = per_tile * i + lax.broadcasted_iota(I32, (nb, tq), 1) // MOBA_BLOCK
    past = blk < own
    geff = jnp.where(past, gate_t, -jnp.inf)
    rank = jnp.zeros((nb, tq), F32)
    for jj in range(nb):
        row = geff[jj:jj + 1, :]
        beats = jnp.where(row > geff, 1.0, jnp.where(row == geff, jnp.where(blk > jj, 1.0, 0.0), 0.0))
        rank = rank + beats
    sel = jnp.where(past, jnp.where(rank < float(MOBA_TOPK), 1.0, 0.0), 0.0)
    weight = jnp.left_shift(jnp.ones((nb, tq), I32), blk).astype(F32)
    selw = (sel * weight).astype(BF16)
    bits = lax.dot_general(selw, jnp.ones((nb, LANES), BF16), (((0,), (0,)), ((), ())),
                           preferred_element_type=F32)
    bits_ref[...] = bits.astype(I32)

    def tile(j):
        off = pl.multiple_of(j * tq, tq)
        s = lax.dot_general(q16_ref[...], k_ref[pl.ds(off, tq), :], _NT, preferred_element_type=F32)
        rep = jnp.concatenate([bits_ref[...]] * (tq // LANES), axis=1)
        kblk = per_tile * j + lax.broadcasted_iota(I32, (tq, tq), 1) // MOBA_BLOCK
        chosen = (lax.shift_right_logical(rep, kblk) & 1) == 1
        return s, chosen, v_ref[pl.ds(off, tq), :]

    s, chosen, v = tile(i)
    r = lax.broadcasted_iota(I32, (tq, tq), 0)
    c = lax.broadcasted_iota(I32, (tq, tq), 1)
    own_blk = (r // MOBA_BLOCK == c // MOBA_BLOCK) & (c <= r)
    _first_step(jnp.where(chosen | own_blk, s, NEG), v, m_ref, acc_ref)

    def body(j, carry):
        s, chosen, v = tile(j)
        _online_step(jnp.where(chosen, s, NEG), v, m_ref, acc_ref)
        return carry

    lax.fori_loop(0, i, body, 0)
    o_ref[...] = _softmax_out(acc_ref).astype(BF16)


def _moba_branch(u_main, mk16, kmean, mv16, batch, seq, tabs):
    tq = ATT_TILE
    nq = seq // tq
    nb = seq // MOBA_BLOCK
    qblk = COL_MQ // LANES
    return pl.pallas_call(
        _moba_kernel,
        grid=(batch, MOBA_HEADS, nq),
        in_specs=[pl.BlockSpec((tq, LANES), lambda b, h, i: (b * nq + i, qblk + h)),
                  pl.BlockSpec((tq, LANES), lambda b, h, i: (i, 0)),
                  pl.BlockSpec((tq, LANES), lambda b, h, i: (i, 0)),
                  pl.BlockSpec((1, 1, nb, LANES), lambda b, h, i: (b, h, 0, 0)),
                  pl.BlockSpec((seq, LANES), lambda b, h, i: (b, h)),
                  pl.BlockSpec((seq, LANES), lambda b, h, i: (b, h))],
        out_specs=pl.BlockSpec((tq, LANES), lambda b, h, i: (b * nq + i, h)),
        out_shape=jax.ShapeDtypeStruct((batch * seq, BRANCH_WIDTH), BF16),
        scratch_shapes=[pltpu.VMEM((tq, LANES), BF16), pltpu.VMEM((tq, LANES), I32),
                        pltpu.VMEM((tq, LANES), F32), pltpu.VMEM((tq, 2 * LANES), F32)],
        compiler_params=_cparams(("parallel", "parallel", "arbitrary"), 32),
        name="moba_attention",
    )(u_main, tabs["c128"], tabs["s128"], kmean, mk16, mv16)


def _mla_kernel(qn_ref, qp_ref, c64_ref, sa_ref, sb_ref, kn_ref, kp_ref, v_ref, o_ref,
                q16_ref, m_ref, acc_ref):
    h = pl.program_id(1)
    i = pl.program_id(2)
    tq = ATT_TILE
    scale = (MLA_NOPE_DIM + MLA_ROPE_DIM) ** -0.5 * LOG2E

    q16_ref[:, :LANES] = (qn_ref[...] * scale).astype(BF16)
    qp = _rope64x2(qp_ref[...], c64_ref[...], sa_ref[...], sb_ref[...]) * scale
    lane = lax.broadcasted_iota(I32, qp.shape, 1)
    q16_ref[:, LANES:] = jnp.where(lane // MLA_ROPE_DIM == h % 2, qp, 0.0).astype(BF16)

    def tile(j):
        off = pl.multiple_of(j * tq, tq)
        kcat = jnp.concatenate([kn_ref[pl.ds(off, tq), :], kp_ref[pl.ds(off, tq), :]], axis=1)
        s = lax.dot_general(q16_ref[...], kcat, _NT, preferred_element_type=F32)
        return s, v_ref[pl.ds(off, tq), :]

    s, v = tile(i)
    _first_step(jnp.where(_causal_mask(tq), s, NEG), v, m_ref, acc_ref)

    def body(j, carry):
        s, v = tile(j)
        _online_step(s, v, m_ref, acc_ref)
        return carry

    lax.fori_loop(0, i, body, 0)
    o_ref[...] = _softmax_out(acc_ref).astype(BF16)


def _mla_branch(q_up, kv_up, kpe, batch, seq, tabs):
    tq = ATT_TILE
    nq = seq // tq

    def tab():
        return pl.BlockSpec((tq, LANES), lambda b, h, i: (i, 0))

    return pl.pallas_call(
        _mla_kernel,
        grid=(batch, MLA_HEADS, nq),
        in_specs=[pl.BlockSpec((tq, LANES), lambda b, h, i: (b * nq + i, h)),
                  pl.BlockSpec((tq, LANES), lambda b, h, i: (b * nq + i, MLA_HEADS + h // 2)),
                  tab(), tab(), tab(),
                  pl.BlockSpec((seq, LANES), lambda b, h, i: (b, h)),
                  pl.BlockSpec((seq, LANES), lambda b, h, i: (b, 0)),
                  pl.BlockSpec((seq, LANES), lambda b, h, i: (b, MLA_HEADS + h))],
        out_specs=pl.BlockSpec((tq, LANES), lambda b, h, i: (b * nq + i, h)),
        out_shape=jax.ShapeDtypeStruct((batch * seq, BRANCH_WIDTH), BF16),
        scratch_shapes=[pltpu.VMEM((tq, 2 * LANES), BF16),
                        pltpu.VMEM((tq, LANES), F32), pltpu.VMEM((tq, 2 * LANES), F32)],
        compiler_params=_cparams(("parallel", "parallel", "arbitrary"), 32),
        name="mla_attention",
    )(q_up, q_up, tabs["c64"], tabs["sa"], tabs["sb"], kv_up, kpe, kv_up)


def _diff_kernel(lam_init, q_ref, c64_ref, sa_ref, sb_ref, k_ref, v_ref, lq1_ref, lk1_ref, lq2_ref, lk2_ref,
                 g_ref, o_ref, q1_ref, q2_ref, m1_ref, a1_ref, m2_ref, a2_ref):
    i = pl.program_id(2)
    tq = ATT_TILE

    q = _rope64x2(q_ref[...], c64_ref[...], sa_ref[...], sb_ref[...]) * (DIFF_HEAD_DIM ** -0.5 * LOG2E)
    lane = lax.broadcasted_iota(I32, q.shape, 1)
    q1_ref[...] = jnp.where(lane < DIFF_HEAD_DIM, q, 0.0).astype(BF16)
    q2_ref[...] = jnp.where(lane < DIFF_HEAD_DIM, 0.0, q).astype(BF16)

    def tile(j):
        off = pl.multiple_of(j * tq, tq)
        k = k_ref[pl.ds(off, tq), :]
        s1 = lax.dot_general(q1_ref[...], k, _NT, preferred_element_type=F32)
        s2 = lax.dot_general(q2_ref[...], k, _NT, preferred_element_type=F32)
        return s1, s2, v_ref[pl.ds(off, tq), :]

    s1, s2, v = tile(i)
    mask = _causal_mask(tq)
    _first_step(jnp.where(mask, s1, NEG), v, m1_ref, a1_ref)
    _first_step(jnp.where(mask, s2, NEG), v, m2_ref, a2_ref)

    def body(j, carry):
        s1, s2, v = tile(j)
        _online_step(s1, v, m1_ref, a1_ref)
        _online_step(s2, v, m2_ref, a2_ref)
        return carry

    lax.fori_loop(0, i, body, 0)
    lam = (jnp.exp(jnp.sum(lq1_ref[...] * lk1_ref[...], axis=-1, keepdims=True))
           - jnp.exp(jnp.sum(lq2_ref[...] * lk2_ref[...], axis=-1, keepdims=True)) + lam_init)
    o = _softmax_out(a1_ref) - lam * _softmax_out(a2_ref)
    o_ref[...] = (_rms_rows(o, g_ref[...]) * (1.0 - lam_init)).astype(BF16)


def _diff_branch(u_main, dk16, dv16, batch, seq, tabs, lq1, lk1, lq2, lk2, subln_g, lam_init):
    tq = ATT_TILE
    nq = seq // tq
    qblk = COL_DQ // LANES

    def tab():
        return pl.BlockSpec((tq, LANES), lambda b, h, i: (i, 0))

    def vec(n):
        return pl.BlockSpec((1, n), lambda b, h, i: (0, 0))

    def kv():
        return pl.BlockSpec((seq, LANES), lambda b, h, i: (b, h))

    stat = [pltpu.VMEM((tq, LANES), F32), pltpu.VMEM((tq, 2 * LANES), F32)]
    return pl.pallas_call(
        functools.partial(_diff_kernel, lam_init),
        grid=(batch, DIFF_HEADS, nq),
        in_specs=[pl.BlockSpec((tq, LANES), lambda b, h, i: (b * nq + i, qblk + h)),
                  tab(), tab(), tab(), kv(), kv(),
                  vec(DIFF_HEAD_DIM), vec(DIFF_HEAD_DIM), vec(DIFF_HEAD_DIM), vec(DIFF_HEAD_DIM),
                  vec(DIFF_V_DIM)],
        out_specs=pl.BlockSpec((tq, LANES), lambda b, h, i: (b * nq + i, h)),
        out_shape=jax.ShapeDtypeStruct((batch * seq, BRANCH_WIDTH), BF16),
        scratch_shapes=[pltpu.VMEM((tq, LANES), BF16), pltpu.VMEM((tq, LANES), BF16)] + stat + stat,
        compiler_params=_cparams(("parallel", "parallel", "arbitrary"), 32),
        name="diff_attention",
    )(u_main, tabs["c64"], tabs["sa"], tabs["sb"], dk16, dv16,
      lq1.reshape(1, -1), lk1.reshape(1, -1), lq2.reshape(1, -1), lk2.reshape(1, -1), subln_g.reshape(1, -1))


def _merge_kernel(h_ref, y_ref, wg_ref, bg_ref, wp_ref, o_ref, acc_ref):
    b = pl.program_id(2)
    gate = _sigmoid(jnp.dot(h_ref[...], wg_ref[0], preferred_element_type=F32) + bg_ref[0])
    term = gate * jnp.dot(y_ref[0], wp_ref[0], preferred_element_type=F32)

    @pl.when(b == 0)
    def _():
        acc_ref[...] = term

    @pl.when(b > 0)
    def _():
        acc_ref[...] += term

    @pl.when(b == N_BRANCHES - 1)
    def _():
        o_ref[...] = acc_ref[...].astype(BF16)


def _gated_merge(h16, ys, w_gate16, b_gate, w_branch16, layer, tm=1024, tn=512):
    t, d = h16.shape
    tm = min(tm, t)
    base = layer * N_BRANCHES
    return pl.pallas_call(
        _merge_kernel,
        grid=(t // tm, d // tn, N_BRANCHES),
        in_specs=[pl.BlockSpec((tm, d), lambda i, j, b: (i, 0)),
                  pl.BlockSpec((1, tm, BRANCH_WIDTH), lambda i, j, b: (b, i, 0)),
                  pl.BlockSpec((1, d, tn), lambda i, j, b: (base + b, 0, j)),
                  pl.BlockSpec((1, 1, tn), lambda i, j, b: (base + b, 0, j)),
                  pl.BlockSpec((1, BRANCH_WIDTH, tn), lambda i, j, b: (base + b, 0, j))],
        out_specs=pl.BlockSpec((tm, tn), lambda i, j, b: (i, j)),
        out_shape=jax.ShapeDtypeStruct((t, d), BF16),
        scratch_shapes=[pltpu.VMEM((tm, tn), F32)],
        compiler_params=_cparams(("parallel", "parallel", "arbitrary"), 48),
        name="gated_merge",
    )(h16, ys, w_gate16, b_gate, w_branch16)


def _router_kernel(h_ref, w_ref, o_ref):
    logits = jnp.dot(h_ref[...], w_ref[...], precision=HI, preferred_element_type=F32)
    lane = lax.broadcasted_iota(I32, logits.shape, 1)
    lanef = lane.astype(F32)
    big = float(LANES)
    is_g = (lane >= MOE_EXPERTS) & (lane < MOE_EXPERTS + MOE_GROUPS)
    gl = jnp.where(is_g, logits, -jnp.inf)
    gmax = jnp.max(gl, axis=-1, keepdims=True)
    gsel = jnp.min(jnp.where(gl == gmax, lanef, big), axis=-1, keepdims=True) - MOE_EXPERTS
    g_w = 1.0 / jnp.sum(jnp.exp(gl - gmax), axis=-1, keepdims=True)
    in_grp = (lane < MOE_EXPERTS) & ((lane // MOE_EXPERTS_PER_GROUP).astype(F32) == gsel)
    el = jnp.where(in_grp, logits, -jnp.inf)
    m1 = jnp.max(el, axis=-1, keepdims=True)
    i1 = jnp.min(jnp.where(el == m1, lanef, big), axis=-1, keepdims=True)
    el2 = jnp.where(lanef == i1, -jnp.inf, el)
    m2 = jnp.max(el2, axis=-1, keepdims=True)
    i2 = jnp.min(jnp.where(el2 == m2, lanef, big), axis=-1, keepdims=True)
    e2 = jnp.exp(m2 - m1)
    w1 = g_w / (1.0 + e2)
    w2 = g_w * e2 / (1.0 + e2)
    o_ref[...] = jnp.where(lane == 0, i1, jnp.where(lane == 1, i2, jnp.where(lane == 2, w1,
                           jnp.where(lane == 3, w2, 0.0))))


def _router(h32, w_router):
    t, d = h32.shape
    return pl.pallas_call(
        _router_kernel,
        grid=(t // ROW_TILE,),
        in_specs=[pl.BlockSpec((ROW_TILE, d), lambda i: (i, 0)), pl.BlockSpec((d, LANES), lambda i: (0, 0))],
        out_specs=pl.BlockSpec((ROW_TILE, LANES), lambda i: (i, 0)),
        out_shape=jax.ShapeDtypeStruct((t, LANES), F32),
        compiler_params=_cparams(("parallel",), 32),
        name="moe_router",
    )(h32, w_router)


def _row_copy(src_hbm, row, dst, dst_row, sem):
    return pltpu.make_async_copy(src_hbm.at[pl.ds(row, 1)], dst.at[pl.ds(dst_row, 1)], sem)


def _gmm_kernel(tile_expert_ref, ntiles_ref, src_ref, h_hbm, wg_ref, wu_ref, wd_ref, o_ref, xbuf, x16_ref, sems):
    i = pl.program_id(0)
    f = pl.program_id(1)
    ntiles = ntiles_ref[0]
    tm = MOE_TILE
    first = f == 0

    def gather(tile, slot, start):
        def body(r, carry):
            cp = _row_copy(h_hbm, src_ref[tile * tm + r], xbuf.at[slot], r, sems.at[slot])
            if start:
                cp.start()
            else:
                cp.wait()
            return carry
        lax.fori_loop(0, tm, body, 0, unroll=8)

    @pl.when(first & (i == 0))
    def _():
        gather(0, 0, True)

    @pl.when(first & (i + 1 < ntiles))
    def _():
        gather(i + 1, (i + 1) % 2, True)

    @pl.when(i < ntiles)
    def _():
        @pl.when(first)
        def _():
            slot = i % 2
            gather(i, slot, False)
            x16_ref[...] = xbuf[slot].astype(BF16)

        x = x16_ref[...]
        a = jnp.dot(x, wg_ref[0].astype(BF16), preferred_element_type=F32)
        u = jnp.dot(x, wu_ref[0].astype(BF16), preferred_element_type=F32)
        hid = (_silu(a) * u).astype(BF16)
        y = jnp.dot(hid, wd_ref[0].astype(BF16), preferred_element_type=F32)

        @pl.when(first)
        def _():
            o_ref[...] = y

        @pl.when(f > 0)
        def _():
            o_ref[...] += y

    @pl.when(first & (i >= ntiles))
    def _():
        o_ref[...] = jnp.zeros(o_ref.shape, F32)


def _grouped_experts(h32, src_tok, tile_expert, ntiles, w_gate, w_up, w_down, layer, max_tiles):
    t, d = h32.shape
    tm = MOE_TILE
    fc = MOE_D_FF // MOE_FF_SPLIT
    base = layer * MOE_EXPERTS

    def expert_chunk(i, f, te, nt):
        last = nt[0] - 1
        return base + te[jnp.minimum(i, last)], jnp.where(i <= last, f, MOE_FF_SPLIT - 1)

    def up_map(i, f, te, nt, src):
        e, c = expert_chunk(i, f, te, nt)
        return (e, 0, c)

    def down_map(i, f, te, nt, src):
        e, c = expert_chunk(i, f, te, nt)
        return (e, c, 0)

    grid_spec = pltpu.PrefetchScalarGridSpec(
        num_scalar_prefetch=3,
        grid=(max_tiles, MOE_FF_SPLIT),
        in_specs=[pl.BlockSpec(memory_space=pl.ANY),
                  pl.BlockSpec((1, d, fc), up_map),
                  pl.BlockSpec((1, d, fc), up_map),
                  pl.BlockSpec((1, fc, d), down_map)],
        out_specs=pl.BlockSpec((tm, d), lambda i, f, te, nt, src: (i, 0)),
        scratch_shapes=[pltpu.VMEM((2, tm, d), F32), pltpu.VMEM((tm, d), BF16), pltpu.SemaphoreType.DMA((2,))],
    )
    return pl.pallas_call(
        _gmm_kernel,
        grid_spec=grid_spec,
        out_shape=jax.ShapeDtypeStruct((max_tiles * tm, d), F32),
        compiler_params=_cparams(("arbitrary", "arbitrary"), 58),
        name="moe_grouped_experts",
    )(tile_expert, ntiles, src_tok, h32, w_gate, w_up, w_down)


def _combine_kernel(dest_ref, h_ref, rw_ref, y_hbm, g_ref, b_ref, o32_ref, o16_ref, ybuf, sems):
    i = pl.program_id(0)
    n = pl.num_programs(0)
    tm = ROW_TILE

    def gather(tile, slot, start):
        def body(r, carry):
            for k in range(MOE_TOPK):
                cp = _row_copy(y_hbm, dest_ref[(tile * tm + r) * MOE_TOPK + k], ybuf.at[slot, k], r,
                               sems.at[slot])
                if start:
                    cp.start()
                else:
                    cp.wait()
            return carry
        lax.fori_loop(0, tm, body, 0, unroll=8)

    @pl.when(i == 0)
    def _():
        gather(0, 0, True)

    @pl.when(i + 1 < n)
    def _():
        gather(i + 1, (i + 1) % 2, True)

    slot = i % 2
    gather(i, slot, False)
    rw = rw_ref[...]
    ffn = rw[:, 2:3] * ybuf[slot, 0] + rw[:, 3:4] * ybuf[slot, 1]
    y = _ln_rows(DEEPNORM_ALPHA * h_ref[...] + ffn, g_ref[...], b_ref[...])
    o32_ref[...] = y
    o16_ref[...] = y.astype(BF16)


def _moe_combine_ln(h32, route, dest, y_sorted, g, b):
    t, d = h32.shape
    tm = ROW_TILE
    row = pl.BlockSpec((tm, d), lambda i, dst: (i, 0))
    vec = pl.BlockSpec((1, d), lambda i, dst: (0, 0))
    grid_spec = pltpu.PrefetchScalarGridSpec(
        num_scalar_prefetch=1,
        grid=(t // tm,),
        in_specs=[row, pl.BlockSpec((tm, LANES), lambda i, dst: (i, 0)),
                  pl.BlockSpec(memory_space=pl.ANY), vec, vec],
        out_specs=[row, row],
        scratch_shapes=[pltpu.VMEM((2, MOE_TOPK, tm, d), F32), pltpu.SemaphoreType.DMA((2,))],
    )
    return pl.pallas_call(
        _combine_kernel,
        grid_spec=grid_spec,
        out_shape=[jax.ShapeDtypeStruct((t, d), F32), jax.ShapeDtypeStruct((t, d), BF16)],
        compiler_params=_cparams(("arbitrary",), 56),
        name="moe_combine_layer_norm",
    )(dest, h32, route, y_sorted, g.reshape(1, d), b.reshape(1, d))


def _moe_plan(route, max_tiles):
    t = route.shape[0]
    tm = MOE_TILE
    expert = route[:, :MOE_TOPK].astype(I32).reshape(-1)
    onehot = (expert[:, None] == jnp.arange(MOE_EXPERTS, dtype=I32)[None, :]).astype(I32)
    rank = jnp.sum((jnp.cumsum(onehot, axis=0) - onehot) * onehot, axis=1)
    counts = jnp.sum(onehot, axis=0)
    tiles_per = (counts + tm - 1) // tm
    tile_end = jnp.cumsum(tiles_per)
    tile_start = tile_end - tiles_per
    dest = tile_start[expert] * tm + rank
    ntiles = tile_end[-1:].astype(I32)
    tile_expert = jnp.minimum(
        jnp.searchsorted(tile_end, jnp.arange(max_tiles, dtype=I32), side="right"), MOE_EXPERTS - 1).astype(I32)
    src_tok = jnp.zeros((max_tiles * tm,), I32).at[dest].set(jnp.arange(MOE_TOPK * t, dtype=I32) // MOE_TOPK)
    return dest.astype(I32), src_tok, tile_expert, ntiles


def _hier_moe_ln(h32, w_router, w_gate, w_up, w_down, layer, ln_g, ln_b):
    t = h32.shape[0]
    max_tiles = (MOE_TOPK * t) // MOE_TILE + MOE_EXPERTS
    route = _router(h32, w_router)
    dest, src_tok, tile_expert, ntiles = _moe_plan(route, max_tiles)
    y_sorted = _grouped_experts(h32, src_tok, tile_expert, ntiles, w_gate, w_up, w_down, layer, max_tiles)
    return _moe_combine_ln(h32, route, dest, y_sorted, ln_g, ln_b)


def _rope_tables(seq):
    def base(dim):
        inv = 1.0 / (ROPE_THETA ** (jnp.arange(0, dim, 2, dtype=F32) / dim))
        ang = jnp.arange(seq, dtype=F32)[:, None] * inv[None, :]
        return jnp.cos(ang), jnp.sin(ang)

    c, s = base(MOBA_HEAD_DIM)
    c6, s6 = base(DIFF_HEAD_DIM)
    z = jnp.zeros_like(s6)
    return {"c128": jnp.concatenate([c, c], axis=1), "s128": jnp.concatenate([-s, s], axis=1),
            "c64": jnp.concatenate([c6] * 4, axis=1),
            "sa": jnp.concatenate([-s6, z, -s6, z], axis=1),
            "sb": jnp.concatenate([z, s6, z, s6], axis=1)}


def _split_w_in(w):
    dt0 = SSM_WIDTH + SSM_CONV_DIM
    mq0 = dt0 + SSM_HEADS
    kr0 = mq0 + 3 * BRANCH_WIDTH + MLA_Q_RANK + MLA_KV_RANK
    dq0 = kr0 + MLA_ROPE_DIM
    main = jnp.concatenate([w[..., :dt0], w[..., mq0:kr0], w[..., dq0:]], axis=-1).astype(BF16)
    pad = jnp.zeros(w.shape[:-1] + (LANES - MLA_ROPE_DIM - SSM_HEADS,), w.dtype)
    tail = jnp.concatenate([w[..., kr0:dq0], w[..., dt0:mq0], pad], axis=-1).astype(BF16)
    return main, tail


def _permute_mla(w_uq, w_ukv):
    depth = w_uq.shape[0]
    hq = MLA_NOPE_DIM + MLA_ROPE_DIM
    uq = w_uq.reshape(depth, MLA_Q_RANK, MLA_HEADS, hq)
    uq = jnp.concatenate([uq[..., :MLA_NOPE_DIM].reshape(depth, MLA_Q_RANK, -1),
                          uq[..., MLA_NOPE_DIM:].reshape(depth, MLA_Q_RANK, -1)], axis=-1).astype(BF16)
    ukv = w_ukv.reshape(depth, MLA_KV_RANK, MLA_HEADS, MLA_NOPE_DIM + MLA_V_DIM)
    ukv = jnp.concatenate([ukv[..., :MLA_NOPE_DIM].reshape(depth, MLA_KV_RANK, -1),
                           ukv[..., MLA_NOPE_DIM:].reshape(depth, MLA_KV_RANK, -1)], axis=-1).astype(BF16)
    return uq, ukv


def kernel(x, emb_ln_g, emb_ln_b, w_in, conv_w, conv_b, dt_bias, a_log, d_skip, ssm_norm_g, mla_q_norm_g, mla_kv_norm_g, mla_w_uq, mla_w_ukv, diff_lambda_q1, diff_lambda_k1, diff_lambda_q2, diff_lambda_k2, diff_subln_g, w_gate, b_gate, w_branch, w_out, ln1_g, ln1_b, router_group, router_expert, moe_w_gate, moe_w_up, moe_w_down, ln2_g, ln2_b):
    batch, seq, d = x.shape
    t = batch * seq
    depth = w_in.shape[0]
    tabs = _rope_tables(seq)
    w_main, w_tail = _split_w_in(w_in)
    w_uq, w_ukv = _permute_mla(mla_w_uq, mla_w_ukv)
    w_gate16 = w_gate.astype(BF16).reshape(depth * N_BRANCHES, d, d)
    w_branch16 = w_branch.astype(BF16).reshape(depth * N_BRANCHES, BRANCH_WIDTH, d)
    b_gate3 = b_gate.reshape(depth * N_BRANCHES, 1, d)
    w_out16 = w_out.astype(BF16)
    moe_g = moe_w_gate.reshape(depth * MOE_EXPERTS, d, MOE_D_FF)
    moe_u = moe_w_up.reshape(depth * MOE_EXPERTS, d, MOE_D_FF)
    moe_d = moe_w_down.reshape(depth * MOE_EXPERTS, MOE_D_FF, d)
    h32, h16 = _layer_norm(x.reshape(t, d), emb_ln_g, emb_ln_b)
    for l in range(depth):
        u_main = _matmul(h16, w_main, l, F32, 512, 1024, "in_proj")
        u_tail = _matmul(h16, w_tail, l, F32, 512, LANES, "in_proj_tail")
        y_ssd = _ssd_branch(u_main, u_tail, batch, seq, conv_w[l], conv_b[l], dt_bias[l], a_log[l],
                            d_skip[l], ssm_norm_g[l])
        mk16, kmean, mv16, cqn, ckvn, kpe, dk16, dv16 = _kv_prep(
            u_main, u_tail, seq, tabs, mla_q_norm_g[l], mla_kv_norm_g[l])
        nb = seq // MOBA_BLOCK
        kmean = kmean.reshape(batch, nb, MOBA_HEADS, LANES).transpose(0, 2, 1, 3)
        y_moba = _moba_branch(u_main, mk16, kmean, mv16, batch, seq, tabs)
        q_up = _matmul(cqn, w_uq, l, F32, 1024, 512, "mla_q_up")
        kv_up = _matmul(ckvn, w_ukv, l, BF16, 1024, 1024, "mla_kv_up")
        y_mla = _mla_branch(q_up, kv_up, kpe, batch, seq, tabs)
        lam_init = 0.8 - 0.6 * math.exp(-0.3 * l)
        y_diff = _diff_branch(u_main, dk16, dv16, batch, seq, tabs, diff_lambda_q1[l], diff_lambda_k1[l],
                              diff_lambda_q2[l], diff_lambda_k2[l], diff_subln_g[l], lam_init)
        ys = jnp.stack([y_ssd, y_moba, y_mla, y_diff])
        merged = _gated_merge(h16, ys, w_gate16, b_gate3, w_branch16, l)
        mix = _matmul(merged, w_out16, l, F32, 1024, 1024, "out_proj")
        h32, h16 = _layer_norm(mix, ln1_g[l], ln1_b[l], residual=h32)
        w_router = jnp.concatenate(
            [router_expert[l], router_group[l],
             jnp.zeros((d, LANES - MOE_EXPERTS - MOE_GROUPS), F32)], axis=1)
        h32, h16 = _hier_moe_ln(h32, w_router, moe_g, moe_u, moe_d, l, ln2_g[l], ln2_b[l])
    return h32.reshape(batch, seq, d)
```

```python
import functools
import math

import jax
import jax.numpy as jnp
import numpy as np
from jax import lax
from jax.experimental import pallas as pl
from jax.experimental.pallas import tpu as pltpu

F32 = jnp.float32
BF16 = jnp.bfloat16
I32 = jnp.int32
HI = lax.Precision.HIGHEST

LANES = 128
SUBLANES = 8
MIB = 1024 * 1024

D_MODEL = 4096
DEPTH = 2
BRANCH_WIDTH = 1024
N_BRANCHES = 4
ROPE_THETA = 10000.0
LN_EPS = 1e-5
RMS_EPS = 1e-6
DEEPNORM_ALPHA = (2 * DEPTH) ** 0.25

SSM_HEADS = 16
SSM_HEAD_DIM = 64
SSM_WIDTH = SSM_HEADS * SSM_HEAD_DIM
SSM_GROUPS = 4
SSM_STATE = 128
SSM_CONV = 4
SSM_CHUNK = 128
SSM_CONV_DIM = SSM_WIDTH + 2 * SSM_GROUPS * SSM_STATE

MOBA_HEADS = 8
MOBA_HEAD_DIM = 128
MOBA_BLOCK = 256
MOBA_TOPK = 3

MLA_HEADS = 8
MLA_Q_RANK = 768
MLA_KV_RANK = 256
MLA_NOPE_DIM = 128
MLA_ROPE_DIM = 64
MLA_V_DIM = 128

DIFF_HEADS = 8
DIFF_HEAD_DIM = 64
DIFF_V_DIM = 2 * DIFF_HEAD_DIM

MOE_GROUPS = 4
MOE_EXPERTS_PER_GROUP = 8
MOE_EXPERTS = MOE_GROUPS * MOE_EXPERTS_PER_GROUP
MOE_TOPK = 2
MOE_D_FF = 512

U_MAIN = 10240
COL_Z, COL_X, COL_BC = 0, 1024, 2048
COL_MQ, COL_MK, COL_MV = 3072, 4096, 5120
COL_CQ, COL_CKV = 6144, 6912
COL_DQ, COL_DK, COL_DV = 7168, 8192, 9216
TAIL_DT = 64

ATT_TILE = 512
ROW_TILE = 256
MOE_TILE = 256
MOE_FF_SPLIT = 1
NEG = -1e30


def _cparams(sem, vmem_mib):
    return pltpu.CompilerParams(dimension_semantics=sem, vmem_limit_bytes=vmem_mib * MIB)


def _silu(x):
    return x * (1.0 / (1.0 + jnp.exp(-x)))


def _sigmoid(x):
    return 1.0 / (1.0 + jnp.exp(-x))


def _rope128(t, cos_f, sin_s):
    return t * cos_f + pltpu.roll(t, 64, 1) * sin_s


def _rope64x2(t, cos_f, sin_a, sin_b):
    return t * cos_f + pltpu.roll(t, 96, 1) * sin_a + pltpu.roll(t, 32, 1) * sin_b


def _ln_rows(x, g, b):
    mu = jnp.mean(x, axis=-1, keepdims=True)
    xc = x - mu
    var = jnp.mean(xc * xc, axis=-1, keepdims=True)
    return xc * lax.rsqrt(var + LN_EPS) * g + b


def _ln_kernel(x_ref, g_ref, b_ref, o32_ref, o16_ref):
    y = _ln_rows(x_ref[...], g_ref[...], b_ref[...])
    o32_ref[...] = y
    o16_ref[...] = y.astype(BF16)


def _res_ln_kernel(h_ref, m_ref, g_ref, b_ref, o32_ref, o16_ref):
    y = _ln_rows(DEEPNORM_ALPHA * h_ref[...] + m_ref[...], g_ref[...], b_ref[...])
    o32_ref[...] = y
    o16_ref[...] = y.astype(BF16)


def _layer_norm(x, g, b, residual=None):
    t, d = x.shape
    row = pl.BlockSpec((ROW_TILE, d), lambda i: (i, 0))
    vec = pl.BlockSpec((1, d), lambda i: (0, 0))
    ins = [x] if residual is None else [residual, x]
    return pl.pallas_call(
        _ln_kernel if residual is None else _res_ln_kernel,
        grid=(t // ROW_TILE,),
        in_specs=[row] * len(ins) + [vec, vec],
        out_specs=[row, row],
        out_shape=[jax.ShapeDtypeStruct((t, d), F32), jax.ShapeDtypeStruct((t, d), BF16)],
        compiler_params=_cparams(("parallel",), 48),
        name="layer_norm" if residual is None else "residual_layer_norm",
    )(*ins, g.reshape(1, d), b.reshape(1, d))


def _mm_kernel(a_ref, b_ref, o_ref):
    o_ref[...] = jnp.dot(a_ref[...], b_ref[0], preferred_element_type=F32).astype(o_ref.dtype)


def _matmul(a, b, layer, out_dtype, tm, tn, name):
    m, k = a.shape
    n = b.shape[2]
    tm, tn = min(tm, m), min(tn, n)
    assert m % tm == 0 and n % tn == 0
    return pl.pallas_call(
        _mm_kernel,
        grid=(n // tn, m // tm),
        in_specs=[pl.BlockSpec((tm, k), lambda j, i: (i, 0)),
                  pl.BlockSpec((1, k, tn), lambda j, i: (layer, 0, j))],
        out_specs=pl.BlockSpec((tm, tn), lambda j, i: (i, j)),
        out_shape=jax.ShapeDtypeStruct((m, n), out_dtype),
        compiler_params=_cparams(("parallel", "parallel"), 48),
        name=name,
    )(a, b)


def _ssd_kernel(z_ref, x_ref, bc_ref, tail_ref, cw_ref, cb_ref, dtb_ref, aneg_ref, dsk_ref, ng_ref,
                o_ref, extx_ref, extbc_ref, state_ref):
    c = pl.program_id(1)
    L = SSM_CHUNK

    @pl.when(c == 0)
    def _():
        extx_ref[0:SUBLANES, :] = jnp.zeros((SUBLANES, SSM_WIDTH), F32)
        extbc_ref[0:SUBLANES, :] = jnp.zeros((SUBLANES, SSM_WIDTH), F32)
        state_ref[...] = jnp.zeros(state_ref.shape, F32)

    def conv_silu(cur_ref, ext_ref, w, b):
        ext_ref[SUBLANES:SUBLANES + L, :] = cur_ref[...]
        acc = b
        for k in range(SSM_CONV):
            off = SUBLANES - (SSM_CONV - 1) + k
            acc = acc + w[k:k + 1, :] * ext_ref[off:off + L, :]
        ext_ref[0:SUBLANES, :] = ext_ref[L:L + SUBLANES, :]
        return _silu(acc)

    cw = cw_ref[...]
    cb = cb_ref[...]
    xs = conv_silu(x_ref, extx_ref, cw[:, :SSM_WIDTH], cb[:, :SSM_WIDTH])
    bc = conv_silu(bc_ref, extbc_ref, cw[:, SSM_WIDTH:], cb[:, SSM_WIDTH:])
    gn = SSM_GROUPS * SSM_STATE

    dtr = tail_ref[...] + dtb_ref[...]
    dt = jnp.maximum(dtr, 0.0) + jnp.log1p(jnp.exp(-jnp.abs(dtr)))
    a = dt * aneg_ref[...]
    ri = lax.broadcasted_iota(I32, (L, L), 0)
    ci = lax.broadcasted_iota(I32, (L, L), 1)
    tril = ri >= ci
    tril_f = tril.astype(F32)
    a_cs = jnp.dot(tril_f, a, precision=HI, preferred_element_type=F32)
    a_cs_t = lax.dot_general(a, (ci >= ri).astype(F32), (((0,), (0,)), ((), ())),
                             precision=HI, preferred_element_type=F32)
    er = lax.broadcasted_iota(I32, (LANES, SSM_WIDTH), 0)
    ec = lax.broadcasted_iota(I32, (LANES, SSM_WIDTH), 1)
    expand = (er - TAIL_DT == ec // SSM_HEAD_DIM).astype(F32)
    dt_e = jnp.dot(dt, expand, precision=HI, preferred_element_type=F32)
    acs_e = jnp.dot(a_cs, expand, precision=HI, preferred_element_type=F32)
    tot_e = acs_e[L - 1:L, :]
    ecs_e = jnp.exp(acs_e)
    etot_e = jnp.exp(tot_e)
    xdt = xs * dt_e
    xd = xdt * jnp.exp(tot_e - acs_e)

    lane = lax.broadcasted_iota(I32, (L, LANES), 1)
    left = lane < SSM_HEAD_DIM
    ys = []
    for g in range(SSM_GROUPS):
        bg = bc[:, g * SSM_STATE:(g + 1) * SSM_STATE].astype(BF16)
        cg = bc[:, gn + g * SSM_STATE:gn + (g + 1) * SSM_STATE].astype(BF16)
        gmat = lax.dot_general(cg, bg, (((1,), (1,)), ((), ())), preferred_element_type=F32)
        for pp in range(2):
            p = 2 * g + pp
            sl = slice(p * LANES, (p + 1) * LANES)
            xdt_p = xdt[:, sl].astype(BF16)
            yh = []
            for h in (2 * p, 2 * p + 1):
                col = a_cs[:, TAIL_DT + h:TAIL_DT + h + 1]
                row = a_cs_t[TAIL_DT + h:TAIL_DT + h + 1, :]
                lmat = jnp.exp(jnp.where(tril, col - row, -jnp.inf))
                yh.append(jnp.dot((gmat * lmat).astype(BF16), xdt_p, preferred_element_type=F32))
            y_diag = jnp.where(left, yh[0], yh[1])
            st = state_ref[p]
            y_off = jnp.dot(cg, st.astype(BF16), preferred_element_type=F32) * ecs_e[:, sl]
            new = lax.dot_general(bg, xd[:, sl].astype(BF16), (((0,), (0,)), ((), ())),
                                  preferred_element_type=F32)
            state_ref[p] = etot_e[:, sl] * st + new
            ys.append(y_diag + y_off + dsk_ref[:, sl] * xs[:, sl])
    y = jnp.concatenate(ys, axis=1) * _silu(z_ref[...])
    ms = jnp.mean(y * y, axis=-1, keepdims=True)
    o_ref[...] = (y * lax.rsqrt(ms + RMS_EPS) * ng_ref[...]).astype(BF16)


def _ssd_branch(u_main, u_tail, batch, seq, conv_w, conv_b, dt_bias, a_log, d_skip, norm_g):
    nc = seq // SSM_CHUNK
    L = SSM_CHUNK

    def lanes16(v):
        return jnp.zeros((1, LANES), F32).at[0, TAIL_DT:TAIL_DT + SSM_HEADS].set(v.astype(F32))

    def col(blk):
        return pl.BlockSpec((L, SSM_WIDTH), lambda b, c: (b * nc + c, blk))

    def vec(n):
        return pl.BlockSpec((1, n), lambda b, c: (0, 0))

    return pl.pallas_call(
        _ssd_kernel,
        grid=(batch, nc),
        in_specs=[col(COL_Z // SSM_WIDTH), col(COL_X // SSM_WIDTH), col(COL_BC // SSM_WIDTH),
                  pl.BlockSpec((L, LANES), lambda b, c: (b * nc + c, 0)),
                  pl.BlockSpec((SSM_CONV, SSM_CONV_DIM), lambda b, c: (0, 0)),
                  vec(SSM_CONV_DIM), vec(LANES), vec(LANES), vec(SSM_WIDTH), vec(SSM_WIDTH)],
        out_specs=pl.BlockSpec((L, SSM_WIDTH), lambda b, c: (b * nc + c, 0)),
        out_shape=jax.ShapeDtypeStruct((batch * seq, SSM_WIDTH), BF16),
        scratch_shapes=[pltpu.VMEM((L + SUBLANES, SSM_WIDTH), F32),
                        pltpu.VMEM((L + SUBLANES, SSM_WIDTH), F32),
                        pltpu.VMEM((SSM_HEADS // 2, SSM_STATE, LANES), F32)],
        compiler_params=_cparams(("parallel", "arbitrary"), 40),
        name="ssd_branch",
    )(u_main, u_main, u_main, u_tail, conv_w, conv_b.reshape(1, -1), lanes16(dt_bias),
      lanes16(-jnp.exp(a_log.astype(F32))), jnp.repeat(d_skip.astype(F32), SSM_HEAD_DIM).reshape(1, -1),
      norm_g.reshape(1, -1))


def _rms_rows(x, g):
    return x * lax.rsqrt(jnp.mean(x * x, axis=-1, keepdims=True) + RMS_EPS) * g


def _kv_prep_kernel(mk_ref, mv_ref, cq_ref, ckv_ref, dk_ref, dv_ref, tail_ref,
                    c128_ref, s128_ref, c64_ref, sa_ref, sb_ref, qg_ref, kvg_ref,
                    mk16_ref, kmean_ref, mv16_ref, cqn_ref, ckvn_ref, kpe_ref, dk16_ref, dv16_ref):
    c128, s128 = c128_ref[...], s128_ref[...]
    c64, sa, sb = c64_ref[...], sa_ref[...], sb_ref[...]
    means = []
    for h in range(MOBA_HEADS):
        sl = slice(h * LANES, (h + 1) * LANES)
        r = _rope128(mk_ref[:, sl], c128, s128)
        mk16_ref[:, sl] = r.astype(BF16)
        means.append(jnp.mean(r, axis=0, keepdims=True))
    kmean_ref[0] = jnp.concatenate(means, axis=0)
    mv16_ref[...] = mv_ref[...].astype(BF16)
    cqn_ref[...] = _rms_rows(cq_ref[...], qg_ref[...]).astype(BF16)
    ckvn_ref[...] = _rms_rows(ckv_ref[...], kvg_ref[...]).astype(BF16)
    r = _rope64x2(tail_ref[...], c64, sa, sb)
    lane = lax.broadcasted_iota(I32, r.shape, 1)
    kpe_ref[...] = jnp.where(lane < MLA_ROPE_DIM, r, pltpu.roll(r, 64, 1)).astype(BF16)
    for h in range(DIFF_HEADS):
        sl = slice(h * LANES, (h + 1) * LANES)
        dk16_ref[:, sl] = _rope64x2(dk_ref[:, sl], c64, sa, sb).astype(BF16)
    dv16_ref[...] = dv_ref[...].astype(BF16)


def _kv_prep(u_main, u_tail, seq, tabs, q_norm_g, kv_norm_g):
    t = u_main.shape[0]
    r = ROW_TILE
    assert r == MOBA_BLOCK
    ns = seq // r

    def col(width, start):
        return pl.BlockSpec((r, width), lambda i: (i, start // width))

    def tab():
        return pl.BlockSpec((r, LANES), lambda i: (i % ns, 0))

    def vec(n):
        return pl.BlockSpec((1, n), lambda i: (0, 0))

    def out(width):
        return pl.BlockSpec((r, width), lambda i: (i, 0))

    return pl.pallas_call(
        _kv_prep_kernel,
        grid=(t // r,),
        in_specs=[col(1024, COL_MK), col(1024, COL_MV), col(MLA_Q_RANK, COL_CQ), col(MLA_KV_RANK, COL_CKV),
                  col(1024, COL_DK), col(1024, COL_DV), pl.BlockSpec((r, LANES), lambda i: (i, 0)),
                  tab(), tab(), tab(), tab(), tab(), vec(MLA_Q_RANK), vec(MLA_KV_RANK)],
        out_specs=[out(1024), pl.BlockSpec((1, MOBA_HEADS, LANES), lambda i: (i, 0, 0)), out(1024),
                   out(MLA_Q_RANK), out(MLA_KV_RANK), out(LANES), out(1024), out(1024)],
        out_shape=[jax.ShapeDtypeStruct((t, 1024), BF16),
                   jax.ShapeDtypeStruct((t // r, MOBA_HEADS, LANES), F32),
                   jax.ShapeDtypeStruct((t, 1024), BF16),
                   jax.ShapeDtypeStruct((t, MLA_Q_RANK), BF16),
                   jax.ShapeDtypeStruct((t, MLA_KV_RANK), BF16),
                   jax.ShapeDtypeStruct((t, LANES), BF16),
                   jax.ShapeDtypeStruct((t, 1024), BF16),
                   jax.ShapeDtypeStruct((t, 1024), BF16)],
        compiler_params=_cparams(("parallel",), 40),
        name="kv_prep",
    )(u_main, u_main, u_main, u_main, u_main, u_main, u_tail,
      tabs["c128"], tabs["s128"], tabs["c64"], tabs["sa"], tabs["sb"],
      q_norm_g.reshape(1, -1), kv_norm_g.reshape(1, -1))


LOG2E = math.log2(math.e)


def _with_ones(v):
    return jnp.concatenate([v, jnp.ones(v.shape, v.dtype)], axis=1)


def _lanes(x, width):
    return jnp.concatenate([x] * (width // LANES), axis=1)


def _online_step(s, v, m_ref, acc_ref):
    m_prev = m_ref[...]
    m_new = jnp.maximum(m_prev, jnp.max(s, axis=-1, keepdims=True))
    alpha = jnp.exp2(m_prev - m_new)
    p = jnp.exp2(s - _lanes(m_new, s.shape[1]))
    acc_ref[...] = _lanes(alpha, acc_ref.shape[1]) * acc_ref[...] + jnp.dot(
        p.astype(BF16), _with_ones(v), preferred_element_type=F32)
    m_ref[...] = m_new


def _first_step(s, v, m_ref, acc_ref):
    m = jnp.broadcast_to(jnp.max(s, axis=-1, keepdims=True), m_ref.shape)
    p = jnp.exp2(s - _lanes(m, s.shape[1]))
    m_ref[...] = m
    acc_ref[...] = jnp.dot(p.astype(BF16), _with_ones(v), preferred_element_type=F32)


def _softmax_out(acc_ref):
    return acc_ref[:, :LANES] / acc_ref[:, LANES:]


def _walk_past_tiles(i, step):
    def body(jj, carry):
        step(2 * jj, 2)
        return carry

    lax.fori_loop(0, i // 2, body, 0)

    @pl.when(i % 2 == 1)
    def _():
        step(i - 1, 1)


_NT = (((1,), (1,)), ((), ()))


def _causal_mask(tq):
    r = lax.broadcasted_iota(I32, (tq, tq), 0)
    c = lax.broadcasted_iota(I32, (tq, tq), 1)
    return c <= r


def _moba_kernel(q_ref, c128_ref, s128_ref, kmean_ref, k_ref, v_ref, o_ref,
                 q16_ref, bits_ref, m_ref, acc_ref):
    i = pl.program_id(2)
    tq = ATT_TILE
    per_tile = tq // MOBA_BLOCK
    nb = kmean_ref.shape[2]

    qf = _rope128(q_ref[...], c128_ref[...], s128_ref[...])
    q16_ref[...] = (qf * (MOBA_HEAD_DIM ** -0.5 * LOG2E)).astype(BF16)
    gate_t = lax.dot_general(kmean_ref[0, 0], qf, _NT, precision=HI, preferred_element_type=F32)
    blk = lax.broadcasted_iota(I32, (nb, tq), 0)
    own = per_tile * i + lax.broadcasted_iota(I32, (nb, tq), 1) // MOBA_BLOCK
    past = blk < own
    geff = jnp.where(past, gate_t, -jnp.inf)
    rank = jnp.zeros((nb, tq), F32)
    for jj in range(nb):
        row = geff[jj:jj + 1, :]
        beats = jnp.where(row > geff, 1.0, jnp.where(row == geff, jnp.where(blk > jj, 1.0, 0.0), 0.0))
        rank = rank + beats
    sel = jnp.where(past, jnp.where(rank < float(MOBA_TOPK), 1.0, 0.0), 0.0)
    weight = jnp.left_shift(jnp.ones((nb, tq), I32), blk).astype(F32)
    selw = (sel * weight).astype(BF16)
    bits = lax.dot_general(selw, jnp.ones((nb, LANES), BF16), (((0,), (0,)), ((), ())),
                           preferred_element_type=F32)
    bits_ref[...] = bits.astype(I32)

    def tile(j, n):
        off = pl.multiple_of(j * tq, tq)
        w = n * tq
        s = lax.dot_general(q16_ref[...], k_ref[pl.ds(off, w), :], _NT, preferred_element_type=F32)
        kblk = per_tile * j + lax.broadcasted_iota(I32, (tq, w), 1) // MOBA_BLOCK
        chosen = (lax.shift_right_logical(_lanes(bits_ref[...], w), kblk) & 1) == 1
        return s, chosen, v_ref[pl.ds(off, w), :]

    s, chosen, v = tile(i, 1)
    r = lax.broadcasted_iota(I32, (tq, tq), 0)
    c = lax.broadcasted_iota(I32, (tq, tq), 1)
    own_blk = (r // MOBA_BLOCK == c // MOBA_BLOCK) & (c <= r)
    _first_step(jnp.where(chosen | own_blk, s, NEG), v, m_ref, acc_ref)

    def step(j, n):
        s, chosen, v = tile(j, n)
        _online_step(jnp.where(chosen, s, NEG), v, m_ref, acc_ref)

    _walk_past_tiles(i, step)
    o_ref[...] = _softmax_out(acc_ref).astype(BF16)


def _moba_branch(u_main, mk16, kmean, mv16, batch, seq, tabs):
    tq = ATT_TILE
    nq = seq // tq
    nb = seq // MOBA_BLOCK
    qblk = COL_MQ // LANES
    return pl.pallas_call(
        _moba_kernel,
        grid=(batch, MOBA_HEADS, nq),
        in_specs=[pl.BlockSpec((tq, LANES), lambda b, h, i: (b * nq + i, qblk + h)),
                  pl.BlockSpec((tq, LANES), lambda b, h, i: (i, 0)),
                  pl.BlockSpec((tq, LANES), lambda b, h, i: (i, 0)),
                  pl.BlockSpec((1, 1, nb, LANES), lambda b, h, i: (b, h, 0, 0)),
                  pl.BlockSpec((seq, LANES), lambda b, h, i: (b, h)),
                  pl.BlockSpec((seq, LANES), lambda b, h, i: (b, h))],
        out_specs=pl.BlockSpec((tq, LANES), lambda b, h, i: (b * nq + i, h)),
        out_shape=jax.ShapeDtypeStruct((batch * seq, BRANCH_WIDTH), BF16),
        scratch_shapes=[pltpu.VMEM((tq, LANES), BF16), pltpu.VMEM((tq, LANES), I32),
                        pltpu.VMEM((tq, LANES), F32), pltpu.VMEM((tq, 2 * LANES), F32)],
        compiler_params=_cparams(("parallel", "parallel", "arbitrary"), 32),
        name="moba_attention",
    )(u_main, tabs["c128"], tabs["s128"], kmean, mk16, mv16)


def _mla_kernel(qn_ref, qp_ref, c64_ref, sa_ref, sb_ref, kn_ref, kp_ref, v_ref, o_ref,
                q16_ref, m_ref, acc_ref):
    h = pl.program_id(1)
    i = pl.program_id(2)
    tq = ATT_TILE
    scale = (MLA_NOPE_DIM + MLA_ROPE_DIM) ** -0.5 * LOG2E

    q16_ref[:, :LANES] = (qn_ref[...] * scale).astype(BF16)
    qp = _rope64x2(qp_ref[...], c64_ref[...], sa_ref[...], sb_ref[...]) * scale
    lane = lax.broadcasted_iota(I32, qp.shape, 1)
    q16_ref[:, LANES:] = jnp.where(lane // MLA_ROPE_DIM == h % 2, qp, 0.0).astype(BF16)

    def tile(j, n):
        off = pl.multiple_of(j * tq, tq)
        w = n * tq
        kcat = jnp.concatenate([kn_ref[pl.ds(off, w), :], kp_ref[pl.ds(off, w), :]], axis=1)
        s = lax.dot_general(q16_ref[...], kcat, _NT, preferred_element_type=F32)
        return s, v_ref[pl.ds(off, w), :]

    s, v = tile(i, 1)
    _first_step(jnp.where(_causal_mask(tq), s, NEG), v, m_ref, acc_ref)

    def step(j, n):
        s, v = tile(j, n)
        _online_step(s, v, m_ref, acc_ref)

    _walk_past_tiles(i, step)
    o_ref[...] = _softmax_out(acc_ref).astype(BF16)


def _mla_branch(q_up, kv_up, kpe, batch, seq, tabs):
    tq = ATT_TILE
    nq = seq // tq

    def tab():
        return pl.BlockSpec((tq, LANES), lambda b, h, i: (i, 0))

    return pl.pallas_call(
        _mla_kernel,
        grid=(batch, MLA_HEADS, nq),
        in_specs=[pl.BlockSpec((tq, LANES), lambda b, h, i: (b * nq + i, h)),
                  pl.BlockSpec((tq, LANES), lambda b, h, i: (b * nq + i, MLA_HEADS + h // 2)),
                  tab(), tab(), tab(),
                  pl.BlockSpec((seq, LANES), lambda b, h, i: (b, h)),
                  pl.BlockSpec((seq, LANES), lambda b, h, i: (b, 0)),
                  pl.BlockSpec((seq, LANES), lambda b, h, i: (b, MLA_HEADS + h))],
        out_specs=pl.BlockSpec((tq, LANES), lambda b, h, i: (b * nq + i, h)),
        out_shape=jax.ShapeDtypeStruct((batch * seq, BRANCH_WIDTH), BF16),
        scratch_shapes=[pltpu.VMEM((tq, 2 * LANES), BF16),
                        pltpu.VMEM((tq, LANES), F32), pltpu.VMEM((tq, 2 * LANES), F32)],
        compiler_params=_cparams(("parallel", "parallel", "arbitrary"), 32),
        name="mla_attention",
    )(q_up, q_up, tabs["c64"], tabs["sa"], tabs["sb"], kv_up, kpe, kv_up)


def _diff_kernel(lam_init, q_ref, c64_ref, sa_ref, sb_ref, k_ref, v_ref, lq1_ref, lk1_ref, lq2_ref, lk2_ref,
                 g_ref, o_ref, q1_ref, q2_ref, m1_ref, a1_ref, m2_ref, a2_ref):
    i = pl.program_id(2)
    tq = ATT_TILE

    q = _rope64x2(q_ref[...], c64_ref[...], sa_ref[...], sb_ref[...]) * (DIFF_HEAD_DIM ** -0.5 * LOG2E)
    lane = lax.broadcasted_iota(I32, q.shape, 1)
    q1_ref[...] = jnp.where(lane < DIFF_HEAD_DIM, q, 0.0).astype(BF16)
    q2_ref[...] = jnp.where(lane < DIFF_HEAD_DIM, 0.0, q).astype(BF16)

    def tile(j, n):
        off = pl.multiple_of(j * tq, tq)
        w = n * tq
        k = k_ref[pl.ds(off, w), :]
        s1 = lax.dot_general(q1_ref[...], k, _NT, preferred_element_type=F32)
        s2 = lax.dot_general(q2_ref[...], k, _NT, preferred_element_type=F32)
        return s1, s2, v_ref[pl.ds(off, w), :]

    s1, s2, v = tile(i, 1)
    mask = _causal_mask(tq)
    _first_step(jnp.where(mask, s1, NEG), v, m1_ref, a1_ref)
    _first_step(jnp.where(mask, s2, NEG), v, m2_ref, a2_ref)

    def step(j, n):
        s1, s2, v = tile(j, n)
        _online_step(s1, v, m1_ref, a1_ref)
        _online_step(s2, v, m2_ref, a2_ref)

    _walk_past_tiles(i, step)
    lam = (jnp.exp(jnp.sum(lq1_ref[...] * lk1_ref[...], axis=-1, keepdims=True))
           - jnp.exp(jnp.sum(lq2_ref[...] * lk2_ref[...], axis=-1, keepdims=True)) + lam_init)
    o = _softmax_out(a1_ref) - lam * _softmax_out(a2_ref)
    o_ref[...] = (_rms_rows(o, g_ref[...]) * (1.0 - lam_init)).astype(BF16)


def _diff_branch(u_main, dk16, dv16, batch, seq, tabs, lq1, lk1, lq2, lk2, subln_g, lam_init):
    tq = ATT_TILE
    nq = seq // tq
    qblk = COL_DQ // LANES

    def tab():
        return pl.BlockSpec((tq, LANES), lambda b, h, i: (i, 0))

    def vec(n):
        return pl.BlockSpec((1, n), lambda b, h, i: (0, 0))

    def kv():
        return pl.BlockSpec((seq, LANES), lambda b, h, i: (b, h))

    stat = [pltpu.VMEM((tq, LANES), F32), pltpu.VMEM((tq, 2 * LANES), F32)]
    return pl.pallas_call(
        functools.partial(_diff_kernel, lam_init),
        grid=(batch, DIFF_HEADS, nq),
        in_specs=[pl.BlockSpec((tq, LANES), lambda b, h, i: (b * nq + i, qblk + h)),
                  tab(), tab(), tab(), kv(), kv(),
                  vec(DIFF_HEAD_DIM), vec(DIFF_HEAD_DIM), vec(DIFF_HEAD_DIM), vec(DIFF_HEAD_DIM),
                  vec(DIFF_V_DIM)],
        out_specs=pl.BlockSpec((tq, LANES), lambda b, h, i: (b * nq + i, h)),
        out_shape=jax.ShapeDtypeStruct((batch * seq, BRANCH_WIDTH), BF16),
        scratch_shapes=[pltpu.VMEM((tq, LANES), BF16), pltpu.VMEM((tq, LANES), BF16)] + stat + stat,
        compiler_params=_cparams(("parallel", "parallel", "arbitrary"), 32),
        name="diff_attention",
    )(u_main, tabs["c64"], tabs["sa"], tabs["sb"], dk16, dv16,
      lq1.reshape(1, -1), lk1.reshape(1, -1), lq2.reshape(1, -1), lk2.reshape(1, -1), subln_g.reshape(1, -1))


def _merge_kernel(h_ref, y_ref, wg_ref, bg_ref, wp_ref, o_ref, acc_ref):
    b = pl.program_id(2)
    gate = _sigmoid(jnp.dot(h_ref[...], wg_ref[0], preferred_element_type=F32) + bg_ref[0])
    term = gate * jnp.dot(y_ref[0], wp_ref[0], preferred_element_type=F32)

    @pl.when(b == 0)
    def _():
        acc_ref[...] = term

    @pl.when(b > 0)
    def _():
        acc_ref[...] += term

    @pl.when(b == N_BRANCHES - 1)
    def _():
        o_ref[...] = acc_ref[...].astype(BF16)


def _gated_merge(h16, ys, w_gate16, b_gate, w_branch16, layer, tm=1024, tn=512):
    t, d = h16.shape
    tm = min(tm, t)
    base = layer * N_BRANCHES
    return pl.pallas_call(
        _merge_kernel,
        grid=(t // tm, d // tn, N_BRANCHES),
        in_specs=[pl.BlockSpec((tm, d), lambda i, j, b: (i, 0)),
                  pl.BlockSpec((1, tm, BRANCH_WIDTH), lambda i, j, b: (b, i, 0)),
                  pl.BlockSpec((1, d, tn), lambda i, j, b: (base + b, 0, j)),
                  pl.BlockSpec((1, 1, tn), lambda i, j, b: (base + b, 0, j)),
                  pl.BlockSpec((1, BRANCH_WIDTH, tn), lambda i, j, b: (base + b, 0, j))],
        out_specs=pl.BlockSpec((tm, tn), lambda i, j, b: (i, j)),
        out_shape=jax.ShapeDtypeStruct((t, d), BF16),
        scratch_shapes=[pltpu.VMEM((tm, tn), F32)],
        compiler_params=_cparams(("parallel", "parallel", "arbitrary"), 48),
        name="gated_merge",
    )(h16, ys, w_gate16, b_gate, w_branch16)


def _router_kernel(h_ref, w_ref, o_ref):
    logits = jnp.dot(h_ref[...], w_ref[...], precision=HI, preferred_element_type=F32)
    lane = lax.broadcasted_iota(I32, logits.shape, 1)
    lanef = lane.astype(F32)
    big = float(LANES)
    is_g = (lane >= MOE_EXPERTS) & (lane < MOE_EXPERTS + MOE_GROUPS)
    gl = jnp.where(is_g, logits, -jnp.inf)
    gmax = jnp.max(gl, axis=-1, keepdims=True)
    gsel = jnp.min(jnp.where(gl == gmax, lanef, big), axis=-1, keepdims=True) - MOE_EXPERTS
    g_w = 1.0 / jnp.sum(jnp.exp(gl - gmax), axis=-1, keepdims=True)
    in_grp = (lane < MOE_EXPERTS) & ((lane // MOE_EXPERTS_PER_GROUP).astype(F32) == gsel)
    el = jnp.where(in_grp, logits, -jnp.inf)
    m1 = jnp.max(el, axis=-1, keepdims=True)
    i1 = jnp.min(jnp.where(el == m1, lanef, big), axis=-1, keepdims=True)
    el2 = jnp.where(lanef == i1, -jnp.inf, el)
    m2 = jnp.max(el2, axis=-1, keepdims=True)
    i2 = jnp.min(jnp.where(el2 == m2, lanef, big), axis=-1, keepdims=True)
    e2 = jnp.exp(m2 - m1)
    w1 = g_w / (1.0 + e2)
    w2 = g_w * e2 / (1.0 + e2)
    o_ref[...] = jnp.where(lane == 0, i1, jnp.where(lane == 1, i2, jnp.where(lane == 2, w1,
                           jnp.where(lane == 3, w2, 0.0))))


def _router(h32, w_router):
    t, d = h32.shape
    return pl.pallas_call(
        _router_kernel,
        grid=(t // ROW_TILE,),
        in_specs=[pl.BlockSpec((ROW_TILE, d), lambda i: (i, 0)), pl.BlockSpec((d, LANES), lambda i: (0, 0))],
        out_specs=pl.BlockSpec((ROW_TILE, LANES), lambda i: (i, 0)),
        out_shape=jax.ShapeDtypeStruct((t, LANES), F32),
        compiler_params=_cparams(("parallel",), 32),
        name="moe_router",
    )(h32, w_router)


def _row_copy(src_hbm, row, dst, dst_row, sem):
    return pltpu.make_async_copy(src_hbm.at[pl.ds(row, 1)], dst.at[pl.ds(dst_row, 1)], sem)


def _gmm_kernel(tile_expert_ref, ntiles_ref, src_ref, h_hbm, wg_ref, wu_ref, wd_ref, o_ref, xbuf, x16_ref, sems):
    i = pl.program_id(0)
    f = pl.program_id(1)
    ntiles = ntiles_ref[0]
    tm = MOE_TILE
    first = f == 0

    def gather(tile, slot, start):
        def body(r, carry):
            cp = _row_copy(h_hbm, src_ref[tile * tm + r], xbuf.at[slot], r, sems.at[slot])
            if start:
                cp.start()
            else:
                cp.wait()
            return carry
        lax.fori_loop(0, tm, body, 0, unroll=8)

    @pl.when(first & (i == 0))
    def _():
        gather(0, 0, True)

    @pl.when(first & (i + 1 < ntiles))
    def _():
        gather(i + 1, (i + 1) % 2, True)

    @pl.when(i < ntiles)
    def _():
        @pl.when(first)
        def _():
            slot = i % 2
            gather(i, slot, False)
            x16_ref[...] = xbuf[slot].astype(BF16)

        x = x16_ref[...]
        a = jnp.dot(x, wg_ref[0].astype(BF16), preferred_element_type=F32)
        u = jnp.dot(x, wu_ref[0].astype(BF16), preferred_element_type=F32)
        hid = (_silu(a) * u).astype(BF16)
        y = jnp.dot(hid, wd_ref[0].astype(BF16), preferred_element_type=F32)

        @pl.when(first)
        def _():
            o_ref[...] = y

        @pl.when(f > 0)
        def _():
            o_ref[...] += y

    @pl.when(first & (i >= ntiles))
    def _():
        o_ref[...] = jnp.zeros(o_ref.shape, F32)


def _grouped_experts(h32, src_tok, tile_expert, ntiles, w_gate, w_up, w_down, layer, max_tiles):
    t, d = h32.shape
    tm = MOE_TILE
    fc = MOE_D_FF // MOE_FF_SPLIT
    base = layer * MOE_EXPERTS

    def expert_chunk(i, f, te, nt):
        last = nt[0] - 1
        return base + te[jnp.minimum(i, last)], jnp.where(i <= last, f, MOE_FF_SPLIT - 1)

    def up_map(i, f, te, nt, src):
        e, c = expert_chunk(i, f, te, nt)
        return (e, 0, c)

    def down_map(i, f, te, nt, src):
        e, c = expert_chunk(i, f, te, nt)
        return (e, c, 0)

    single = pl.Buffered(1)

    grid_spec = pltpu.PrefetchScalarGridSpec(
        num_scalar_prefetch=3,
        grid=(max_tiles, MOE_FF_SPLIT),
        in_specs=[pl.BlockSpec(memory_space=pl.ANY),
                  pl.BlockSpec((1, d, fc), up_map, pipeline_mode=single),
                  pl.BlockSpec((1, d, fc), up_map, pipeline_mode=single),
                  pl.BlockSpec((1, fc, d), down_map, pipeline_mode=single)],
        out_specs=pl.BlockSpec((tm, d), lambda i, f, te, nt, src: (i, 0)),
        scratch_shapes=[pltpu.VMEM((2, tm, d), F32), pltpu.VMEM((tm, d), BF16), pltpu.SemaphoreType.DMA((2,))],
    )
    return pl.pallas_call(
        _gmm_kernel,
        grid_spec=grid_spec,
        out_shape=jax.ShapeDtypeStruct((max_tiles * tm, d), F32),
        compiler_params=_cparams(("arbitrary", "arbitrary"), 58),
        name="moe_grouped_experts",
    )(tile_expert, ntiles, src_tok, h32, w_gate, w_up, w_down)


def _combine_kernel(dest_ref, h_ref, rw_ref, y_hbm, g_ref, b_ref, o32_ref, o16_ref, ybuf, sems):
    i = pl.program_id(0)
    n = pl.num_programs(0)
    tm = ROW_TILE

    def gather(tile, slot, start):
        def body(r, carry):
            for k in range(MOE_TOPK):
                cp = _row_copy(y_hbm, dest_ref[(tile * tm + r) * MOE_TOPK + k], ybuf.at[slot, k], r,
                               sems.at[slot])
                if start:
                    cp.start()
                else:
                    cp.wait()
            return carry
        lax.fori_loop(0, tm, body, 0, unroll=8)

    @pl.when(i == 0)
    def _():
        gather(0, 0, True)

    @pl.when(i + 1 < n)
    def _():
        gather(i + 1, (i + 1) % 2, True)

    slot = i % 2
    gather(i, slot, False)
    rw = rw_ref[...]
    ffn = rw[:, 2:3] * ybuf[slot, 0] + rw[:, 3:4] * ybuf[slot, 1]
    y = _ln_rows(DEEPNORM_ALPHA * h_ref[...] + ffn, g_ref[...], b_ref[...])
    o32_ref[...] = y
    o16_ref[...] = y.astype(BF16)


def _moe_combine_ln(h32, route, dest, y_sorted, g, b):
    t, d = h32.shape
    tm = ROW_TILE
    row = pl.BlockSpec((tm, d), lambda i, dst: (i, 0))
    vec = pl.BlockSpec((1, d), lambda i, dst: (0, 0))
    grid_spec = pltpu.PrefetchScalarGridSpec(
        num_scalar_prefetch=1,
        grid=(t // tm,),
        in_specs=[row, pl.BlockSpec((tm, LANES), lambda i, dst: (i, 0)),
                  pl.BlockSpec(memory_space=pl.ANY), vec, vec],
        out_specs=[row, row],
        scratch_shapes=[pltpu.VMEM((2, MOE_TOPK, tm, d), F32), pltpu.SemaphoreType.DMA((2,))],
    )
    return pl.pallas_call(
        _combine_kernel,
        grid_spec=grid_spec,
        out_shape=[jax.ShapeDtypeStruct((t, d), F32), jax.ShapeDtypeStruct((t, d), BF16)],
        compiler_params=_cparams(("arbitrary",), 56),
        name="moe_combine_layer_norm",
    )(dest, h32, route, y_sorted, g.reshape(1, d), b.reshape(1, d))


def _moe_plan(route, max_tiles):
    t = route.shape[0]
    tm = MOE_TILE
    expert = route[:, :MOE_TOPK].astype(I32).reshape(-1)
    onehot = (expert[:, None] == jnp.arange(MOE_EXPERTS, dtype=I32)[None, :]).astype(I32)
    rank = jnp.sum((jnp.cumsum(onehot, axis=0) - onehot) * onehot, axis=1)
    counts = jnp.sum(onehot, axis=0)
    tiles_per = (counts + tm - 1) // tm
    tile_end = jnp.cumsum(tiles_per)
    tile_start = tile_end - tiles_per
    dest = tile_start[expert] * tm + rank
    ntiles = tile_end[-1:].astype(I32)
    tile_expert = jnp.minimum(
        jnp.searchsorted(tile_end, jnp.arange(max_tiles, dtype=I32), side="right"), MOE_EXPERTS - 1).astype(I32)
    src_tok = jnp.zeros((max_tiles * tm,), I32).at[dest].set(jnp.arange(MOE_TOPK * t, dtype=I32) // MOE_TOPK)
    return dest.astype(I32), src_tok, tile_expert, ntiles


def _hier_moe_ln(h32, w_router, w_gate, w_up, w_down, layer, ln_g, ln_b):
    t = h32.shape[0]
    max_tiles = (MOE_TOPK * t) // MOE_TILE + MOE_EXPERTS
    route = _router(h32, w_router)
    dest, src_tok, tile_expert, ntiles = _moe_plan(route, max_tiles)
    y_sorted = _grouped_experts(h32, src_tok, tile_expert, ntiles, w_gate, w_up, w_down, layer, max_tiles)
    return _moe_combine_ln(h32, route, dest, y_sorted, ln_g, ln_b)


def _rope_tables(seq):
    def base(dim):
        inv = 1.0 / (ROPE_THETA ** (jnp.arange(0, dim, 2, dtype=F32) / dim))
        ang = jnp.arange(seq, dtype=F32)[:, None] * inv[None, :]
        return jnp.cos(ang), jnp.sin(ang)

    c, s = base(MOBA_HEAD_DIM)
    c6, s6 = base(DIFF_HEAD_DIM)
    z = jnp.zeros_like(s6)
    return {"c128": jnp.concatenate([c, c], axis=1), "s128": jnp.concatenate([-s, s], axis=1),
            "c64": jnp.concatenate([c6] * 4, axis=1),
            "sa": jnp.concatenate([-s6, z, -s6, z], axis=1),
            "sb": jnp.concatenate([z, s6, z, s6], axis=1)}


def _split_w_in(w):
    dt0 = SSM_WIDTH + SSM_CONV_DIM
    mq0 = dt0 + SSM_HEADS
    kr0 = mq0 + 3 * BRANCH_WIDTH + MLA_Q_RANK + MLA_KV_RANK
    dq0 = kr0 + MLA_ROPE_DIM
    main = jnp.concatenate([w[..., :dt0], w[..., mq0:kr0], w[..., dq0:]], axis=-1).astype(BF16)
    pad = jnp.zeros(w.shape[:-1] + (LANES - MLA_ROPE_DIM - SSM_HEADS,), w.dtype)
    tail = jnp.concatenate([w[..., kr0:dq0], w[..., dt0:mq0], pad], axis=-1).astype(BF16)
    return main, tail


def _permute_mla(w_uq, w_ukv):
    depth = w_uq.shape[0]
    hq = MLA_NOPE_DIM + MLA_ROPE_DIM
    uq = w_uq.reshape(depth, MLA_Q_RANK, MLA_HEADS, hq)
    uq = jnp.concatenate([uq[..., :MLA_NOPE_DIM].reshape(depth, MLA_Q_RANK, -1),
                          uq[..., MLA_NOPE_DIM:].reshape(depth, MLA_Q_RANK, -1)], axis=-1).astype(BF16)
    ukv = w_ukv.reshape(depth, MLA_KV_RANK, MLA_HEADS, MLA_NOPE_DIM + MLA_V_DIM)
    ukv = jnp.concatenate([ukv[..., :MLA_NOPE_DIM].reshape(depth, MLA_KV_RANK, -1),
                           ukv[..., MLA_NOPE_DIM:].reshape(depth, MLA_KV_RANK, -1)], axis=-1).astype(BF16)
    return uq, ukv


def kernel(x, emb_ln_g, emb_ln_b, w_in, conv_w, conv_b, dt_bias, a_log, d_skip, ssm_norm_g, mla_q_norm_g, mla_kv_norm_g, mla_w_uq, mla_w_ukv, diff_lambda_q1, diff_lambda_k1, diff_lambda_q2, diff_lambda_k2, diff_subln_g, w_gate, b_gate, w_branch, w_out, ln1_g, ln1_b, router_group, router_expert, moe_w_gate, moe_w_up, moe_w_down, ln2_g, ln2_b):
    batch, seq, d = x.shape
    t = batch * seq
    depth = w_in.shape[0]
    tabs = _rope_tables(seq)
    w_main, w_tail = _split_w_in(w_in)
    w_uq, w_ukv = _permute_mla(mla_w_uq, mla_w_ukv)
    w_gate16 = w_gate.astype(BF16).reshape(depth * N_BRANCHES, d, d)
    w_branch16 = w_branch.astype(BF16).reshape(depth * N_BRANCHES, BRANCH_WIDTH, d)
    b_gate3 = b_gate.reshape(depth * N_BRANCHES, 1, d)
    w_out16 = w_out.astype(BF16)
    moe_g = moe_w_gate.reshape(depth * MOE_EXPERTS, d, MOE_D_FF)
    moe_u = moe_w_up.reshape(depth * MOE_EXPERTS, d, MOE_D_FF)
    moe_d = moe_w_down.reshape(depth * MOE_EXPERTS, MOE_D_FF, d)
    h32, h16 = _layer_norm(x.reshape(t, d), emb_ln_g, emb_ln_b)
    for l in range(depth):
        u_main = _matmul(h16, w_main, l, F32, 512, 1024, "in_proj")
        u_tail = _matmul(h16, w_tail, l, F32, 512, LANES, "in_proj_tail")
        y_ssd = _ssd_branch(u_main, u_tail, batch, seq, conv_w[l], conv_b[l], dt_bias[l], a_log[l],
                            d_skip[l], ssm_norm_g[l])
        mk16, kmean, mv16, cqn, ckvn, kpe, dk16, dv16 = _kv_prep(
            u_main, u_tail, seq, tabs, mla_q_norm_g[l], mla_kv_norm_g[l])
        nb = seq // MOBA_BLOCK
        kmean = kmean.reshape(batch, nb, MOBA_HEADS, LANES).transpose(0, 2, 1, 3)
        y_moba = _moba_branch(u_main, mk16, kmean, mv16, batch, seq, tabs)
        q_up = _matmul(cqn, w_uq, l, F32, 1024, 512, "mla_q_up")
        kv_up = _matmul(ckvn, w_ukv, l, BF16, 1024, 1024, "mla_kv_up")
        y_mla = _mla_branch(q_up, kv_up, kpe, batch, seq, tabs)
        lam_init = 0.8 - 0.6 * math.exp(-0.3 * l)
        y_diff = _diff_branch(u_main, dk16, dv16, batch, seq, tabs, diff_lambda_q1[l], diff_lambda_k1[l],
                              diff_lambda_q2[l], diff_lambda_k2[l], diff_subln_g[l], lam_init)
        ys = jnp.stack([y_ssd, y_moba, y_mla, y_diff])
        merged = _gated_merge(h16, ys, w_gate16, b_gate3, w_branch16, l)
        mix = _matmul(merged, w_out16, l, F32, 1024, 1024, "out_proj")
        h32, h16 = _layer_norm(mix, ln1_g[l], ln1_b[l], residual=h32)
        w_router = jnp.concatenate(
            [router_expert[l], router_group[l],
             jnp.zeros((d, LANES - MOE_EXPERTS - MOE_GROUPS), F32)], axis=1)
        h32, h16 = _hier_moe_ln(h32, w_router, moe_g, moe_u, moe_d, l, ln2_g[l], ln2_b[l])
    return h32.reshape(batch, seq, d)
```
